```python
import math
import jax, jax.numpy as jnp
from jax import lax
import numpy as np

D_MODEL = 2048
BATCH = 32
SEQ = 256
DEPTH = 2
DEC_BATCH = 2
DEC_SEQ = 1024
PAST_LEN = 256

GRID_W = 64
Q_BLOCK = 128
CHUNK = 128
GROUP_W = D_MODEL // 4
MIX_W = 4 * GROUP_W
DV_A = 128
DK_A = DV_A // 2
H_A = GROUP_W // DV_A
B_CH = 128
G_B = GROUP_W // B_CH
HD_C = 128
H_C = GROUP_W // HD_C
KV_C = H_C // 2
POOL_WINDOWS = (2, 4, 8, 16)
G_D = 4
CD = GROUP_W // G_D
SPLITS = (H_A * 2 * DK_A, H_A * 2 * DK_A, H_A * DV_A, GROUP_W, GROUP_W, H_C * HD_C, KV_C * HD_C, KV_C * HD_C, GROUP_W)
PROJ_W = 2 * H_A * 2 * DK_A + H_A * DV_A + 2 * GROUP_W + H_C * HD_C + 2 * KV_C * HD_C + GROUP_W
N_EXPERTS = 32
TOP_K = 4
D_FF = D_MODEL
MOE_BLOCK = 128
SWIGLU_LIMIT = 7.0
SWIGLU_ALPHA = 1.702
ROPE_THETA = 10000.0
EPS = 1e-6
N_MOD = 6

kernel_name = 'hybrid_diffusion_prefix_step'


def rms_norm(x, g):
    xf = x.astype(jnp.float32)
    y = xf * lax.rsqrt(jnp.mean(xf * xf, axis=-1, keepdims=True) + EPS)
    return (y * g.astype(jnp.float32)).astype(x.dtype)


def lambda_init(l):
    return 0.8 - 0.6 * math.exp(-0.3 * l)


def grid_positions(n):
    rows = n // GRID_W
    row_idx = jnp.repeat(jnp.arange(rows), GRID_W)
    col_idx = jnp.tile(jnp.arange(GRID_W), rows)
    return row_idx.astype(jnp.float32), col_idx.astype(jnp.float32)


def rope_cos_sin(n, rot_dim):
    quarter = rot_dim // 4
    inv = ROPE_THETA ** (-jnp.arange(quarter, dtype=jnp.float32) / quarter)
    row, col = grid_positions(n)
    ang = jnp.stack([row[:, None] * inv, col[:, None] * inv], axis=1)
    return jnp.cos(ang), jnp.sin(ang)


def apply_rope(x, cos, sin):
    B, n, H, Dr = x.shape
    xr = x.reshape(B, n, H, 2, 2, Dr // 4).astype(jnp.float32)
    a, b = xr[..., 0, :], xr[..., 1, :]
    c, s = cos[None, :, None], sin[None, :, None]
    out = jnp.stack([a * c - b * s, b * c + a * s], axis=-2)
    return out.reshape(B, n, H, Dr).astype(x.dtype)


def sweep_query_blocks(fn, *qs):
    B, S = qs[0].shape[:2]
    nb = S // Q_BLOCK
    blocks = tuple(jnp.moveaxis(q.reshape(B, nb, Q_BLOCK, *q.shape[2:]), 1, 0) for q in qs)
    out = lax.map(lambda args: fn(*args), blocks)
    return jnp.moveaxis(out, 0, 1).reshape(B, S, *out.shape[3:])


def diff_attention(q, k, v, lam):
    scale = DK_A ** -0.5
    def block(qb):
        s = jnp.einsum('bqhjd,bnhjd->bhjqn', qb, k).astype(jnp.float32) * scale
        p = jax.nn.softmax(s, axis=-1)
        a = p[:, :, 0] - lam * p[:, :, 1]
        return jnp.einsum('bhqn,bnhd->bqhd', a.astype(v.dtype), v)
    return sweep_query_blocks(block, q)


def gqa_attention(q, k, v):
    B, S = q.shape[:2]
    qg = q.reshape(B, S, KV_C, H_C // KV_C, HD_C)
    scale = HD_C ** -0.5
    def block(qb):
        s = jnp.einsum('bqkgd,bnkd->bkgqn', qb, k).astype(jnp.float32) * scale
        p = jax.nn.softmax(s, axis=-1)
        return jnp.einsum('bkgqn,bnkd->bqkgd', p.astype(v.dtype), v)
    o = sweep_query_blocks(block, qg)
    return o.reshape(B, S, H_C * HD_C)


def chunk_gating(u, v, g_v, w_s, b_s):
    B, n, _ = u.shape
    vn = rms_norm(v.reshape(B, n, G_B, B_CH), g_v.reshape(G_B, B_CH))
    vc = vn.reshape(B, n // CHUNK, CHUNK, G_B, B_CH)
    mixed = jnp.einsum('gpq,bcqgd->bcpgd', w_s, vc) + b_s.T[None, None, :, :, None]
    return u * mixed.reshape(B, n, GROUP_W)


def multiscale_pool(p, w_d, s_d):
    B, n, _ = p.shape
    x = p.reshape(B, n, G_D, CD)
    xf = x.astype(jnp.float32)
    cs = jnp.concatenate([jnp.zeros((B, 1, G_D, CD), jnp.float32), jnp.cumsum(xf, axis=1)], axis=1)
    t = jnp.arange(n)[:, None]
    left = np.array([w // 2 for w in POOL_WINDOWS])
    right = np.array([w - 1 - w // 2 for w in POOL_WINDOWS])
    start = jnp.clip(t - left, 0, n)
    end = jnp.clip(t + right + 1, 0, n)
    gidx = jnp.arange(G_D)[None, :]
    window_sum = cs[:, end, gidx] - cs[:, start, gidx]
    mean = window_sum / (end - start).astype(jnp.float32)[None, :, :, None]
    pooled = (mean - xf).astype(p.dtype)
    y = jnp.einsum('bngc,gcd->bngd', pooled, w_d)
    return y.reshape(B, n, GROUP_W) * s_d


def moe_ffn(h, l, p):
    T, D = h.shape
    logits = (h @ p['w_router'][l] + p['b_router'][l]).astype(jnp.float32)
    top_val, top_idx = lax.top_k(logits, TOP_K)
    gates = jax.nn.softmax(top_val, axis=-1)
    flat_e = top_idx.reshape(-1)
    flat_tok = jnp.repeat(jnp.arange(T, dtype=jnp.int32), TOP_K)
    flat_g = gates.reshape(-1)
    order = jnp.argsort(flat_e)
    sorted_e = flat_e[order]
    counts = jnp.bincount(flat_e, length=N_EXPERTS)
    starts = jnp.cumsum(counts) - counts
    padded = (counts + MOE_BLOCK - 1) // MOE_BLOCK * MOE_BLOCK
    padded_ends = jnp.cumsum(padded)
    padded_starts = padded_ends - padded
    dest = padded_starts[sorted_e] + jnp.arange(T * TOP_K) - starts[sorted_e]
    n_blocks = -(-(T * TOP_K + N_EXPERTS * (MOE_BLOCK - 1)) // MOE_BLOCK)
    n_slots = n_blocks * MOE_BLOCK
    slot_tok = jnp.full((n_slots,), T, jnp.int32).at[dest].set(flat_tok[order])
    slot_gate = jnp.zeros((n_slots,), jnp.float32).at[dest].set(flat_g[order])
    block_expert = jnp.minimum(jnp.searchsorted(padded_ends, jnp.arange(n_blocks) * MOE_BLOCK, side='right'), N_EXPERTS - 1)
    h_pad = jnp.concatenate([h, jnp.zeros((1, D), h.dtype)], axis=0)
    xb = h_pad[slot_tok].reshape(n_blocks, MOE_BLOCK, D)
    def expert_block(args):
        xblk, e = args
        g = xblk @ p['w_gate'][l, e] + p['b_gate'][l, e]
        u = xblk @ p['w_up'][l, e] + p['b_up'][l, e]
        g = jnp.minimum(g, SWIGLU_LIMIT)
        u = jnp.clip(u, -SWIGLU_LIMIT, SWIGLU_LIMIT)
        act = (u + 1.0) * g * jax.nn.sigmoid(SWIGLU_ALPHA * g)
        return act @ p['w_down'][l, e] + p['b_down'][l, e]
    yb = lax.map(expert_block, (xb, block_expert))
    y = jnp.zeros((T + 1, D), jnp.float32).at[slot_tok].add(yb.reshape(n_slots, D).astype(jnp.float32) * slot_gate[:, None])
    return y[:T].astype(h.dtype)


def run_layer(x, cond, l, p, ctx=None, rope=None):
    B, n, D = x.shape
    mod = jnp.einsum('bd,de->be', jax.nn.silu(cond), p['w_mod'][l]) + p['b_mod'][l]
    sh1, sc1, gt1, sh2, sc2, gt2 = [m[:, None, :] for m in jnp.split(mod, N_MOD, axis=-1)]
    h = rms_norm(x, p['g_norm1'][l]) * (1.0 + sc1) + sh1
    proj = h @ p['w_in'][l]
    qa, ka, va, ub, vb, qc, kc, vc, pd = jnp.split(proj, list(np.cumsum(SPLITS)[:-1]), axis=-1)
    qa = qa.reshape(B, n, H_A, 2, DK_A)
    ka = ka.reshape(B, n, H_A, 2, DK_A)
    va = va.reshape(B, n, H_A, DV_A)
    qc = rms_norm(qc.reshape(B, n, H_C, HD_C), p['g_q_c'][l])
    kc = rms_norm(kc.reshape(B, n, KV_C, HD_C), p['g_k_c'][l])
    vc = vc.reshape(B, n, KV_C, HD_C)
    if ctx is None:
        state = (ka, va, kc, vc)
        ka_all, va_all, kc_all, vc_all = ka, va, kc, vc
    else:
        state = None
        cos_a, sin_a, cos_c, sin_c = rope
        qa = apply_rope(qa.reshape(B, n, H_A * 2, DK_A), cos_a, sin_a).reshape(B, n, H_A, 2, DK_A)
        ka = apply_rope(ka.reshape(B, n, H_A * 2, DK_A), cos_a, sin_a).reshape(B, n, H_A, 2, DK_A)
        qc = apply_rope(qc, cos_c, sin_c)
        kc = apply_rope(kc, cos_c, sin_c)
        ka_all = jnp.concatenate([ka, ctx[0]], axis=1)
        va_all = jnp.concatenate([va, ctx[1]], axis=1)
        kc_all = jnp.concatenate([kc, ctx[2]], axis=1)
        vc_all = jnp.concatenate([vc, ctx[3]], axis=1)
    lam0 = lambda_init(l)
    lam = (jnp.exp(jnp.sum(p['lam_q1'][l] * p['lam_k1'][l]).astype(jnp.float32))
           - jnp.exp(jnp.sum(p['lam_q2'][l] * p['lam_k2'][l]).astype(jnp.float32)) + lam0)
    oa = diff_attention(qa, ka_all, va_all, lam)
    oa = (rms_norm(oa, p['g_subln_a'][l]) * (1.0 - lam0)).reshape(B, n, H_A * DV_A)
    ob = chunk_gating(ub, vb, p['g_v_b'][l], p['w_s_b'][l], p['b_s_b'][l])
    oc = gqa_attention(qc, kc_all, vc_all)
    od = multiscale_pool(pd, p['w_d'][l], p['s_d'][l])
    mix = jnp.concatenate([oa, ob, oc, od], axis=-1)
    x = x + gt1 * (mix @ p['w_out'][l])
    h = rms_norm(x, p['g_norm2'][l]) * (1.0 + sc2) + sh2
    x = x + gt2 * moe_ffn(h.reshape(B * n, D), l, p).reshape(B, n, D)
    return x, state


def setup_inputs(seed: int = 0) -> dict:
    key = jax.random.key(seed)
    ks = jax.random.split(key, 40)
    f32 = jnp.float32
    L = DEPTH
    def nrm(i, shape, scale):
        return jax.random.normal(ks[i], shape, f32) * scale
    def gain(i, shape):
        return 1.0 + nrm(i, shape, 0.02)
    return {
        'x_prompt': nrm(0, (BATCH, SEQ, D_MODEL), 1.0),
        'x_sample': nrm(1, (DEC_BATCH, DEC_SEQ, D_MODEL), 1.0),
        'c': nrm(2, (DEC_BATCH, D_MODEL), 1.0),
        'cache_k_a': nrm(3, (DEC_BATCH, L, PAST_LEN, H_A, 2, DK_A), 1.0),
        'cache_v_a': nrm(4, (DEC_BATCH, L, PAST_LEN, H_A, DV_A), 1.0),
        'cache_k_c': nrm(5, (DEC_BATCH, L, PAST_LEN, KV_C, HD_C), 1.0),
        'cache_v_c': nrm(6, (DEC_BATCH, L, PAST_LEN, KV_C, HD_C), 1.0),
        'c_ctx': nrm(7, (D_MODEL,), 1.0),
        'w_mod': nrm(8, (L, D_MODEL, N_MOD * D_MODEL), 0.5 * D_MODEL ** -0.5),
        'b_mod': nrm(9, (L, N_MOD * D_MODEL), 0.02),
        'g_norm1': gain(10, (L, D_MODEL)),
        'g_norm2': gain(11, (L, D_MODEL)),
        'w_in': nrm(12, (L, D_MODEL, PROJ_W), D_MODEL ** -0.5),
        'lam_q1': nrm(13, (L, DK_A), 0.1),
        'lam_k1': nrm(14, (L, DK_A), 0.1),
        'lam_q2': nrm(15, (L, DK_A), 0.1),
        'lam_k2': nrm(16, (L, DK_A), 0.1),
        'g_subln_a': gain(17, (L, DV_A)),
        'g_q_c': gain(18, (L, HD_C)),
        'g_k_c': gain(19, (L, HD_C)),
        'g_v_b': gain(20, (L, GROUP_W)),
        'w_s_b': nrm(21, (L, G_B, CHUNK, CHUNK), CHUNK ** -0.5),
        'b_s_b': gain(22, (L, G_B, CHUNK)),
        'w_d': nrm(23, (L, G_D, CD, CD), CD ** -0.5),
        's_d': gain(24, (L, GROUP_W)),
        'w_out': nrm(25, (L, MIX_W, D_MODEL), MIX_W ** -0.5),
        'w_router': nrm(26, (L, D_MODEL, N_EXPERTS), D_MODEL ** -0.5),
        'b_router': nrm(27, (L, N_EXPERTS), 0.01),
        'w_gate': nrm(28, (L, N_EXPERTS, D_MODEL, D_FF), D_MODEL ** -0.5),
        'b_gate': nrm(29, (L, N_EXPERTS, D_FF), 0.01),
        'w_up': nrm(30, (L, N_EXPERTS, D_MODEL, D_FF), D_MODEL ** -0.5),
        'b_up': nrm(31, (L, N_EXPERTS, D_FF), 0.01),
        'w_down': nrm(32, (L, N_EXPERTS, D_FF, D_MODEL), D_FF ** -0.5),
        'b_down': nrm(33, (L, N_EXPERTS, D_MODEL), 0.01),
        'g_final': gain(34, (D_MODEL,)),
    }


def reference(x_prompt, x_sample, c, cache_k_a, cache_v_a, cache_k_c, cache_v_c, c_ctx,
              w_mod, b_mod, g_norm1, g_norm2, w_in, lam_q1, lam_k1, lam_q2, lam_k2,
              g_subln_a, g_q_c, g_k_c, g_v_b, w_s_b, b_s_b, w_d, s_d, w_out,
              w_router, b_router, w_gate, b_gate, w_up, b_up, w_down, b_down, g_final):
    p = {'w_mod': w_mod, 'b_mod': b_mod, 'g_norm1': g_norm1, 'g_norm2': g_norm2, 'w_in': w_in,
         'lam_q1': lam_q1, 'lam_k1': lam_k1, 'lam_q2': lam_q2, 'lam_k2': lam_k2,
         'g_subln_a': g_subln_a, 'g_q_c': g_q_c, 'g_k_c': g_k_c, 'g_v_b': g_v_b,
         'w_s_b': w_s_b, 'b_s_b': b_s_b, 'w_d': w_d, 's_d': s_d, 'w_out': w_out,
         'w_router': w_router, 'b_router': b_router, 'w_gate': w_gate, 'b_gate': b_gate,
         'w_up': w_up, 'b_up': b_up, 'w_down': w_down, 'b_down': b_down}

    xp = x_prompt
    cond_ctx = c_ctx[None, :]
    states = []
    for l in range(DEPTH):
        xp, st = run_layer(xp, cond_ctx, l, p)
        states.append(st)
    y_prompt = rms_norm(xp, g_final)
    new_k_a = jnp.stack([s[0] for s in states], axis=1)
    new_v_a = jnp.stack([s[1] for s in states], axis=1)
    new_k_c = jnp.stack([s[2] for s in states], axis=1)
    new_v_c = jnp.stack([s[3] for s in states], axis=1)

    n = x_sample.shape[1]
    cos_a, sin_a = rope_cos_sin(n, DK_A)
    cos_c, sin_c = rope_cos_sin(n, HD_C)
    rope = (cos_a, sin_a, cos_c, sin_c)
    xs = x_sample
    for l in range(DEPTH):
        ctx = (cache_k_a[:, l], cache_v_a[:, l], cache_k_c[:, l], cache_v_c[:, l])
        xs, _ = run_layer(xs, c, l, p, ctx, rope)
    y_sample = rms_norm(xs, g_final)
    return (y_prompt, y_sample, new_k_a, new_v_a, new_k_c, new_v_c)
```

```python
import functools
import math

import jax
import jax.numpy as jnp
from jax import lax
from jax.experimental import pallas as pl
from jax.experimental.pallas import tpu as pltpu

F32 = jnp.float32
BF16 = jnp.bfloat16

EPS = 1e-6
GRID_W = 64
ROPE_THETA = 10000.0
N_MOD = 6
TOP_K = 4
SWIGLU_LIMIT = 7.0
SWIGLU_ALPHA = 1.702
POOL_WINDOWS = (2, 4, 8, 16)

LANES = 128
QB = 128
MOE_ROWS = 128
SB_BLOCKS = 16
VMEM_LIMIT = 56 * 1024 * 1024


def _cparams(sem, vmem=VMEM_LIMIT):
    return pltpu.CompilerParams(dimension_semantics=sem, vmem_limit_bytes=vmem)


def _dot(a, b):
    return jnp.dot(a, b, preferred_element_type=F32)


def _dot_nt(a, b):
    return lax.dot_general(a, b, (((1,), (1,)), ((), ())), preferred_element_type=F32)


def _rms_lanes(x, g):
    ms = jnp.mean(x * x, axis=-1, keepdims=True)
    return x * lax.rsqrt(ms + EPS) * g


def _mod_kernel(cond_ref, w_ref, b_ref, o_ref):
    c = cond_ref[...]
    a = (c * jax.nn.sigmoid(c)).astype(BF16)
    o_ref[0] = _dot(a, w_ref[0].astype(BF16)) + b_ref[0]


def _modulation(cond8, w_mod, b_mod):
    L, D, NM = w_mod.shape
    tn = 1024
    return pl.pallas_call(
        _mod_kernel,
        out_shape=jax.ShapeDtypeStruct((L, 8, NM), F32),
        grid=(L, NM // tn),
        in_specs=[
            pl.BlockSpec((8, D), lambda l, n: (0, 0)),
            pl.BlockSpec((1, D, tn), lambda l, n: (l, 0, n)),
            pl.BlockSpec((1, 1, tn), lambda l, n: (l, 0, n)),
        ],
        out_specs=pl.BlockSpec((1, 8, tn), lambda l, n: (l, 0, n)),
        compiler_params=_cparams(("arbitrary", "arbitrary")),
        name="modulation",
    )(cond8, w_mod, b_mod.reshape(L, 1, NM))


def _cond_index(row_block, rows_per_block, n_ctx_rows, dec_seq):
    r0 = row_block * rows_per_block
    return jnp.where(r0 < n_ctx_rows, 0, 1 + (r0 - n_ctx_rows) // dec_seq)


def _inproj_kernel(x_ref, g_ref, mods_ref, w_ref, o_ref, h_ref, *, shift_row, scale_row, chunk):
    @pl.when(pl.program_id(1) == 0)
    def _():
        m = mods_ref[0]
        shift = m[shift_row:shift_row + 1]
        scale1 = 1.0 + m[scale_row:scale_row + 1]
        g = g_ref[...]

        def body(i, carry):
            r = pl.multiple_of(i * chunk, chunk)
            x = x_ref[pl.ds(r, chunk), :]
            h_ref[pl.ds(r, chunk), :] = (_rms_lanes(x, g) * scale1 + shift).astype(BF16)
            return carry

        lax.fori_loop(0, x_ref.shape[0] // chunk, body, 0)

    o_ref[...] = _dot(h_ref[...], w_ref[0].astype(BF16))


def _inproj(x, g, mods, w_in, l, n_ctx_rows, dec_seq):
    T, D = x.shape
    PW = w_in.shape[2]
    tm, tn = 1024, 512
    kern = functools.partial(_inproj_kernel, shift_row=0, scale_row=1, chunk=128)
    return pl.pallas_call(
        kern,
        out_shape=jax.ShapeDtypeStruct((T, PW), F32),
        grid=(T // tm, PW // tn),
        in_specs=[
            pl.BlockSpec((tm, D), lambda m, n: (m, 0)),
            pl.BlockSpec((1, D), lambda m, n: (0, 0)),
            pl.BlockSpec((1, N_MOD, D), lambda m, n: (_cond_index(m, tm, n_ctx_rows, dec_seq), 0, 0)),
            pl.BlockSpec((1, D, tn), lambda m, n: (l, 0, n)),
        ],
        out_specs=pl.BlockSpec((tm, tn), lambda m, n: (m, n)),
        scratch_shapes=[pltpu.VMEM((tm, D), BF16)],
        compiler_params=_cparams(("arbitrary", "arbitrary")),
        name="inproj",
    )(x, g, mods, w_in)


def _rope(x, cos, sin, q):
    lane = lax.broadcasted_iota(jnp.int32, x.shape, 1)
    first_half = (lane & q) == 0
    partner = jnp.where(first_half, pltpu.roll(x, LANES - q, 1), pltpu.roll(x, q, 1))
    return x * cos + partner * sin


def _softmax_parts(s):
    m = jnp.max(s, axis=-1, keepdims=True)
    e = jnp.exp(s - m)
    return e, jnp.sum(e, axis=-1, keepdims=True)


def _mixer_kernel(*refs, seq, n_cache, lam0, rope):
    it = iter(refs)
    qa_ref, ub_ref, vb_ref, qc_ref = next(it), next(it), next(it), next(it)
    ka_ref, va_ref, kv_ref, pd_ref = next(it), next(it), next(it), next(it)
    if n_cache:
        cka_ref, cva_ref, ckc_ref, cvc_ref = next(it), next(it), next(it), next(it)
    if rope:
        cos_a_ref, sin_a_ref, cos_c_ref, sin_c_ref = next(it), next(it), next(it), next(it)
    lamp_ref, gsub_ref, gq_ref, gk_ref, gvb_ref = next(it), next(it), next(it), next(it), next(it)
    ws_ref, bst_ref, wd_ref, sd_ref = next(it), next(it), next(it), next(it)
    mix_ref = next(it)
    kcn_ref = None if n_cache else next(it)
    ka_s, va_s, kc_s, vc_s = next(it), next(it), next(it), next(it)

    j = pl.program_id(1)
    row0 = pl.multiple_of(j * QB, QB)
    n_heads = qa_ref.shape[1] // LANES
    n_kv = kc_s.shape[1] // LANES
    dk_a = LANES // 2

    @pl.when(j == 0)
    def _():
        prep = 256
        for c in range(seq // prep):
            rows = slice(c * prep, (c + 1) * prep)
            for b in range(n_heads):
                cols = slice(b * LANES, (b + 1) * LANES)
                k = ka_ref[rows, cols]
                if rope:
                    k = _rope(k, cos_a_ref[rows, :], sin_a_ref[rows, :], dk_a // 4)
                ka_s[rows, cols] = k.astype(BF16)
                va_s[rows, cols] = va_ref[rows, cols].astype(BF16)
            for b in range(n_kv):
                cols = slice(b * LANES, (b + 1) * LANES)
                k = _rms_lanes(kv_ref[rows, cols], gk_ref[...])
                if kcn_ref is not None:
                    kcn_ref[rows, cols] = k
                if rope:
                    k = _rope(k, cos_c_ref[rows, :], sin_c_ref[rows, :], LANES // 4)
                kc_s[rows, cols] = k.astype(BF16)
                vcols = slice((n_kv + b) * LANES, (n_kv + b + 1) * LANES)
                vc_s[rows, cols] = kv_ref[rows, vcols].astype(BF16)
        if n_cache:
            tail = slice(seq, seq + n_cache)
            ka_s[tail, :] = cka_ref[0].astype(BF16)
            va_s[tail, :] = cva_ref[0].astype(BF16)
            kc_s[tail, :] = ckc_ref[0].astype(BF16)
            vc_s[tail, :] = cvc_ref[0].astype(BF16)

    lane = lax.broadcasted_iota(jnp.int32, (QB, LANES), 1)

    lp = lamp_ref[...]
    lam = (jnp.exp(jnp.sum(lp[0:1] * lp[1:2], axis=-1, keepdims=True))
           - jnp.exp(jnp.sum(lp[2:3] * lp[3:4], axis=-1, keepdims=True)) + lam0)
    scale_a = dk_a ** -0.5
    if rope:
        cos_a = cos_a_ref[pl.ds(row0, QB), :]
        sin_a = sin_a_ref[pl.ds(row0, QB), :]
        cos_c = cos_c_ref[pl.ds(row0, QB), :]
        sin_c = sin_c_ref[pl.ds(row0, QB), :]
    for h in range(n_heads):
        cols = slice(h * LANES, (h + 1) * LANES)
        q = qa_ref[:, cols]
        if rope:
            q = _rope(q, cos_a, sin_a, dk_a // 4)
        k = ka_s[:, cols]
        q0 = jnp.where(lane < dk_a, q, 0.0).astype(BF16)
        q1 = jnp.where(lane >= dk_a, q, 0.0).astype(BF16)
        e0, l0 = _softmax_parts(_dot_nt(q0, k) * scale_a)
        e1, l1 = _softmax_parts(_dot_nt(q1, k) * scale_a)
        a = e0 * (1.0 / l0) - e1 * (lam / l1)
        o = _dot(a.astype(BF16), va_s[:, cols])
        mix_ref[:, cols] = (_rms_lanes(o, gsub_ref[...]) * (1.0 - lam0)).astype(BF16)

    off_b = n_heads * LANES
    for g in range(ub_ref.shape[1] // LANES):
        cols = slice(g * LANES, (g + 1) * LANES)
        vn = _rms_lanes(vb_ref[:, cols], gvb_ref[:, cols])
        mixed = _dot(ws_ref[g].astype(BF16), vn.astype(BF16)) + bst_ref[:, g:g + 1]
        mix_ref[:, off_b + g * LANES: off_b + (g + 1) * LANES] = (ub_ref[:, cols] * mixed).astype(BF16)

    off_c = off_b + ub_ref.shape[1]
    scale_c = LANES ** -0.5
    for h in range(n_heads):
        cols = slice(h * LANES, (h + 1) * LANES)
        q = _rms_lanes(qc_ref[:, cols], gq_ref[...])
        if rope:
            q = _rope(q, cos_c, sin_c, LANES // 4)
        kv = h // (n_heads // n_kv)
        kcols = slice(kv * LANES, (kv + 1) * LANES)
        e, l = _softmax_parts(_dot_nt(q.astype(BF16), kc_s[:, kcols]) * scale_c)
        p = e * (1.0 / l)
        o = _dot(p.astype(BF16), vc_s[:, kcols])
        mix_ref[:, off_c + h * LANES: off_c + (h + 1) * LANES] = o.astype(BF16)

    off_d = off_c + qc_ref.shape[1]
    win = min(3 * QB, seq)
    start = pl.multiple_of(jnp.clip(row0 - QB, 0, seq - win), QB)
    t = row0 + lax.broadcasted_iota(jnp.int32, (QB, win), 0)
    col = start + lax.broadcasted_iota(jnp.int32, (QB, win), 1)
    t1 = row0 + lax.broadcasted_iota(jnp.int32, (QB, 1), 0)
    for g, w in enumerate(POOL_WINDOWS):
        cols = slice(g * LANES, (g + 1) * LANES)
        left, right = w // 2, w - 1 - w // 2
        band = jnp.where((col >= t - left) & (col <= t + right), 1.0, 0.0).astype(BF16)
        cnt = (jnp.minimum(t1 + right + 1, seq) - jnp.maximum(t1 - left, 0)).astype(F32)
        xw = pd_ref[pl.ds(start, win), cols]
        hi = xw.astype(BF16)
        lo = (xw - hi.astype(F32)).astype(BF16)
        wsum = _dot(band, hi) + _dot(band, lo)
        xg = pd_ref[pl.ds(row0, QB), cols]
        pooled = wsum / cnt - xg
        y = _dot(pooled.astype(BF16), wd_ref[g].astype(BF16)) * sd_ref[:, cols]
        mix_ref[:, off_d + g * LANES: off_d + (g + 1) * LANES] = y.astype(BF16)


def _mixer(proj, l, row_start, n_seq, seq, lam0, params, caches=None, rope_tabs=None):
    T, PW = proj.shape
    GW = PW // 8
    n_qb = seq // QB
    qb0 = row_start // QB
    sb0 = row_start // seq
    n_cache = 0 if caches is None else caches[0].shape[2]
    n_keys = seq + n_cache

    def qspec(col):
        return pl.BlockSpec((QB, GW), lambda s, j: (qb0 + s * n_qb + j, col))

    def sspec(col):
        return pl.BlockSpec((seq, GW), lambda s, j: (sb0 + s, col))

    def full(shape):
        nd = len(shape)
        return pl.BlockSpec(shape, lambda s, j: (0,) * nd)

    in_specs = [qspec(0), qspec(3), qspec(4), qspec(5), sspec(1), sspec(2), sspec(6), sspec(7)]
    args = [proj] * 8
    if caches is not None:
        for c in caches:
            in_specs.append(pl.BlockSpec((None, 1, c.shape[2], c.shape[3]), lambda s, j: (s, l, 0, 0)))
            args.append(c)
    if rope_tabs is not None:
        for tab in rope_tabs:
            in_specs.append(full(tab.shape))
            args.append(tab)
    for p in params:
        in_specs.append(full(p.shape))
        args.append(p)

    MW = 4 * GW
    mix_spec = pl.BlockSpec((QB, MW), lambda s, j: (s * n_qb + j, 0))
    if caches is None:
        out_shape = (jax.ShapeDtypeStruct((n_seq * seq, MW), BF16),
                     jax.ShapeDtypeStruct((n_seq * seq, GW // 2), F32))
        out_specs = (mix_spec, pl.BlockSpec((seq, GW // 2), lambda s, j: (s, 0)))
    else:
        out_shape = jax.ShapeDtypeStruct((n_seq * seq, MW), BF16)
        out_specs = mix_spec

    kern = functools.partial(_mixer_kernel, seq=seq, n_cache=n_cache, lam0=lam0,
                             rope=rope_tabs is not None)
    return pl.pallas_call(
        kern,
        out_shape=out_shape,
        grid=(n_seq, n_qb),
        in_specs=in_specs,
        out_specs=out_specs,
        scratch_shapes=[pltpu.VMEM((n_keys, GW), BF16), pltpu.VMEM((n_keys, GW), BF16),
                        pltpu.VMEM((n_keys, GW // 2), BF16), pltpu.VMEM((n_keys, GW // 2), BF16)],
        compiler_params=_cparams(("arbitrary", "arbitrary")),
        name="mixer_latent" if caches is not None else "mixer_context",
    )(*args)


def _outproj_kernel(mix_c_ref, mix_s_ref, w_ref, x_ref, mods_ref, o_ref, *, gate_row, n_ctx_tiles):
    tn = o_ref.shape[1]
    gcols = pl.ds(pl.multiple_of(pl.program_id(1) * tn, tn), tn)
    gate = mods_ref[0, gate_row:gate_row + 1, gcols]
    w = w_ref[0].astype(BF16)
    is_ctx = pl.program_id(0) < n_ctx_tiles

    @pl.when(is_ctx)
    def _():
        o_ref[...] = x_ref[...] + gate * _dot(mix_c_ref[...], w)

    @pl.when(jnp.logical_not(is_ctx))
    def _():
        o_ref[...] = x_ref[...] + gate * _dot(mix_s_ref[...], w)


def _outproj(mix_c, mix_s, w_out, x, mods, l, n_ctx_rows, dec_seq):
    T, D = x.shape
    MW = mix_c.shape[1]
    tm, tn = 1024, 512
    nct = n_ctx_rows // tm
    kern = functools.partial(_outproj_kernel, gate_row=2, n_ctx_tiles=nct)
    return pl.pallas_call(
        kern,
        out_shape=jax.ShapeDtypeStruct((T, D), F32),
        grid=(T // tm, D // tn),
        in_specs=[
            pl.BlockSpec((tm, MW), lambda m, n: (jnp.minimum(m, nct - 1), 0)),
            pl.BlockSpec((tm, MW), lambda m, n: (jnp.maximum(m - nct, 0), 0)),
            pl.BlockSpec((1, MW, tn), lambda m, n: (l, 0, n)),
            pl.BlockSpec((tm, tn), lambda m, n: (m, n)),
            pl.BlockSpec((1, N_MOD, D), lambda m, n: (_cond_index(m, tm, n_ctx_rows, dec_seq), 0, 0)),
        ],
        out_specs=pl.BlockSpec((tm, tn), lambda m, n: (m, n)),
        compiler_params=_cparams(("arbitrary", "arbitrary")),
        name="outproj",
    )(mix_c, mix_s, w_out, x, mods)


def _split_bf16(x):
    hi = x.astype(BF16)
    return hi, (x - hi.astype(F32)).astype(BF16)


def _router_kernel(x_ref, g_ref, mods_ref, wr_ref, br_ref, h_ref, idx_ref, gate_ref, *, n_experts):
    m = mods_ref[0]
    h = _rms_lanes(x_ref[...], g_ref[...]) * (1.0 + m[4:5]) + m[3:4]
    h_ref[...] = h
    h_hi, h_lo = _split_bf16(h)
    w_hi, w_lo = _split_bf16(wr_ref[...])
    logits = _dot(h_hi, w_hi) + _dot(h_hi, w_lo) + _dot(h_lo, w_hi) + br_ref[...]
    lane = lax.broadcasted_iota(jnp.int32, logits.shape, 1)
    lane_f = lane.astype(F32)
    neg = jnp.float32(-jnp.inf)
    logits = jnp.where(lane < n_experts, logits, neg)
    idx_out = jnp.zeros(logits.shape, F32)
    val_out = jnp.zeros(logits.shape, F32)
    top = None
    denom = None
    for k in range(TOP_K):
        v = jnp.max(logits, axis=-1, keepdims=True)
        i = jnp.min(jnp.where(logits == v, lane_f, float(LANES)), axis=-1, keepdims=True)
        if k == 0:
            top = v
        e = jnp.exp(v - top)
        denom = e if denom is None else denom + e
        idx_out = jnp.where(lane == k, i, idx_out)
        val_out = jnp.where(lane == k, e, val_out)
        logits = jnp.where(lane_f == i, neg, logits)
    idx_ref[...] = idx_out.astype(jnp.int32)
    gate_ref[...] = val_out / denom


def _router(x, g, mods, w_router_l, b_router_l, n_ctx_rows, dec_seq):
    T, D = x.shape
    E = w_router_l.shape[1]
    tm = 512
    wr = jnp.zeros((D, LANES), F32).at[:, :E].set(w_router_l)
    br = jnp.zeros((1, LANES), F32).at[0, :E].set(b_router_l)
    kern = functools.partial(_router_kernel, n_experts=E)
    return pl.pallas_call(
        kern,
        out_shape=(jax.ShapeDtypeStruct((T, D), F32),
                   jax.ShapeDtypeStruct((T, LANES), jnp.int32),
                   jax.ShapeDtypeStruct((T, LANES), F32)),
        grid=(T // tm,),
        in_specs=[
            pl.BlockSpec((tm, D), lambda m: (m, 0)),
            pl.BlockSpec((1, D), lambda m: (0, 0)),
            pl.BlockSpec((1, N_MOD, D), lambda m: (_cond_index(m, tm, n_ctx_rows, dec_seq), 0, 0)),
            pl.BlockSpec((D, LANES), lambda m: (0, 0)),
            pl.BlockSpec((1, LANES), lambda m: (0, 0)),
        ],
        out_specs=(pl.BlockSpec((tm, D), lambda m: (m, 0)),
                   pl.BlockSpec((tm, LANES), lambda m: (m, 0)),
                   pl.BlockSpec((tm, LANES), lambda m: (m, 0))),
        compiler_params=_cparams(("arbitrary",)),
        name="norm_router",
    )(x, g, mods, wr, br)


def _route_plan(top_idx, n_experts, n_sb):
    T, K = top_idx.shape
    flat_e = top_idx.reshape(-1)
    onehot = (flat_e[:, None] == jnp.arange(n_experts, dtype=jnp.int32)[None, :]).astype(jnp.int32)
    csum = jnp.cumsum(onehot, axis=0)
    counts = csum[-1]
    rank = jnp.take_along_axis(csum, flat_e[:, None], axis=1)[:, 0] - 1
    nblk = (counts + MOE_ROWS - 1) // MOE_ROWS
    blk_start = jnp.cumsum(nblk) - nblk
    pos = blk_start[flat_e] * MOE_ROWS + rank
    n_slots = -(-(T * K + n_experts * (MOE_ROWS - 1)) // MOE_ROWS) * MOE_ROWS
    tok = jnp.arange(T * K, dtype=jnp.int32) // K
    slot_tok = jnp.zeros((n_slots,), jnp.int32).at[pos].set(tok)
    nsb = (nblk + SB_BLOCKS - 1) // SB_BLOCKS
    sb_end = jnp.cumsum(nsb)
    s = jnp.arange(n_sb, dtype=jnp.int32)
    total = sb_end[-1]
    sc = jnp.minimum(s, total - 1)
    e = jnp.searchsorted(sb_end, sc, side="right").astype(jnp.int32)
    k = sc - (sb_end[e] - nsb[e])
    sb_blk0 = (blk_start[e] + k * SB_BLOCKS).astype(jnp.int32)
    sb_nblk = jnp.where(s < total, jnp.minimum(SB_BLOCKS, nblk[e] - k * SB_BLOCKS), 0).astype(jnp.int32)
    n_used = jnp.sum(nblk).astype(jnp.int32).reshape(1)
    return pos.reshape(T, K).astype(jnp.int32), slot_tok, (e, sb_blk0, sb_nblk, n_used)


def _gather_kernel(tok_ref, h_hbm, o_ref, stage, sem):
    n = o_ref.shape[0]

    def row_copy(i):
        t = tok_ref[0, 0, i]
        return pltpu.make_async_copy(h_hbm.at[pl.ds(t, 1)], stage.at[pl.ds(i, 1)], sem)

    def issue(i, c):
        row_copy(i).start()
        return c

    def drain(i, c):
        row_copy(i).wait()
        return c

    lax.fori_loop(0, n, issue, 0)
    lax.fori_loop(0, n, drain, 0)
    o_ref[...] = stage[...].astype(BF16)


def _gather_rows(h, slot_tok):
    T, D = h.shape
    n_slots = slot_tok.shape[0]
    gb = 256
    return pl.pallas_call(
        _gather_kernel,
        out_shape=jax.ShapeDtypeStruct((n_slots, D), BF16),
        grid=(n_slots // gb,),
        in_specs=[
            pl.BlockSpec((1, 1, gb), lambda i: (i, 0, 0), memory_space=pltpu.SMEM),
            pl.BlockSpec(memory_space=pl.ANY),
        ],
        out_specs=pl.BlockSpec((gb, D), lambda i: (i, 0)),
        scratch_shapes=[pltpu.VMEM((gb, D), F32), pltpu.SemaphoreType.DMA],
        compiler_params=_cparams(("arbitrary",)),
        name="moe_gather",
    )(slot_tok.reshape(n_slots // gb, 1, gb), h)


def _moe_kernel(sb_e_ref, sb_blk0_ref, sb_nblk_ref, n_used_ref, xs_hbm, wg_ref, bg_ref, wu_ref, bu_ref,
                wd_ref, bd_ref, y_hbm, x_s, y_s, wgu_s, wd_s, sem_in, sem_out):
    s = pl.program_id(0)
    f = pl.program_id(1)
    n_f = pl.num_programs(1)
    nblk = sb_nblk_ref[s]
    blk0 = sb_blk0_ref[s]
    tf = wg_ref.shape[3]

    def in_copy(r):
        src = xs_hbm.at[pl.ds(pl.multiple_of((blk0 + r) * MOE_ROWS, MOE_ROWS), MOE_ROWS)]
        return pltpu.make_async_copy(src, x_s.at[pl.ds(pl.multiple_of(r * MOE_ROWS, MOE_ROWS), MOE_ROWS)], sem_in)

    def out_copy(r):
        dst = y_hbm.at[pl.ds(pl.multiple_of((blk0 + r) * MOE_ROWS, MOE_ROWS), MOE_ROWS)]
        return pltpu.make_async_copy(y_s.at[pl.ds(pl.multiple_of(r * MOE_ROWS, MOE_ROWS), MOE_ROWS)], dst, sem_out)

    def each_block(fn):
        def body(r, c):
            fn(r)
            return c
        lax.fori_loop(0, nblk, body, 0)

    @pl.when(nblk > 0)
    def _():
        @pl.when(f == 0)
        def _():
            each_block(lambda r: in_copy(r).start())

        wgu_s[:, :tf] = wg_ref[0, 0].astype(BF16)
        wgu_s[:, tf:] = wu_ref[0, 0].astype(BF16)
        wd_s[...] = wd_ref[0, 0].astype(BF16)
        bg = bg_ref[0, 0]
        bu = bu_ref[0, 0]
        bd = bd_ref[0, 0]

        @pl.when(f == 0)
        def _():
            each_block(lambda r: in_copy(r).wait())

        def ffn(r):
            rows = pl.ds(pl.multiple_of(r * MOE_ROWS, MOE_ROWS), MOE_ROWS)
            gu = _dot(x_s[rows, :], wgu_s[...])
            g = jnp.minimum(gu[:, :tf] + bg, SWIGLU_LIMIT)
            u = jnp.clip(gu[:, tf:] + bu, -SWIGLU_LIMIT, SWIGLU_LIMIT)
            act = (u + 1.0) * g * jax.nn.sigmoid(SWIGLU_ALPHA * g)
            y = _dot(act.astype(BF16), wd_s[...])

            @pl.when(f == 0)
            def _():
                y_s[rows, :] = y + bd

            @pl.when(f > 0)
            def _():
                y_s[rows, :] += y

        each_block(ffn)

        @pl.when(f == n_f - 1)
        def _():
            each_block(lambda r: out_copy(r).start())
            each_block(lambda r: out_copy(r).wait())

    @pl.when(jnp.logical_and(s == pl.num_programs(0) - 1, f == n_f - 1))
    def _():
        y_s[0:MOE_ROWS, :] = jnp.zeros((MOE_ROWS, y_s.shape[1]), F32)

        def tail_copy(r):
            dst = y_hbm.at[pl.ds(pl.multiple_of(r * MOE_ROWS, MOE_ROWS), MOE_ROWS)]
            return pltpu.make_async_copy(y_s.at[0:MOE_ROWS], dst, sem_out)

        def start(r, c):
            tail_copy(r).start()
            return c

        def wait(r, c):
            tail_copy(r).wait()
            return c

        lax.fori_loop(n_used_ref[0], y_hbm.shape[0] // MOE_ROWS, start, 0)
        lax.fori_loop(n_used_ref[0], y_hbm.shape[0] // MOE_ROWS, wait, 0)


def _moe_ffn(xs, plan, w_gate, b_gate, w_up, b_up, w_down, b_down, l, n_sb):
    sb_e, sb_blk0, sb_nblk, n_used = plan
    n_slots, D = xs.shape
    L, E, _, F = w_gate.shape
    tf = 256
    rows = SB_BLOCKS * MOE_ROWS
    n_f = F // tf

    def ftile(s, f, n):
        return jnp.where(n[s] > 0, f, n_f - 1)

    grid_spec = pltpu.PrefetchScalarGridSpec(
        num_scalar_prefetch=4,
        grid=(n_sb, n_f),
        in_specs=[
            pl.BlockSpec(memory_space=pl.ANY),
            pl.BlockSpec((1, 1, D, tf), lambda s, f, e, b, n, u: (l, e[s], 0, ftile(s, f, n))),
            pl.BlockSpec((1, 1, 1, tf), lambda s, f, e, b, n, u: (l, e[s], 0, ftile(s, f, n))),
            pl.BlockSpec((1, 1, D, tf), lambda s, f, e, b, n, u: (l, e[s], 0, ftile(s, f, n))),
            pl.BlockSpec((1, 1, 1, tf), lambda s, f, e, b, n, u: (l, e[s], 0, ftile(s, f, n))),
            pl.BlockSpec((1, 1, tf, D), lambda s, f, e, b, n, u: (l, e[s], ftile(s, f, n), 0)),
            pl.BlockSpec((1, 1, 1, D), lambda s, f, e, b, n, u: (l, e[s], 0, 0)),
        ],
        out_specs=pl.BlockSpec(memory_space=pl.ANY),
        scratch_shapes=[
            pltpu.VMEM((rows, D), BF16),
            pltpu.VMEM((rows, D), F32),
            pltpu.VMEM((D, 2 * tf), BF16),
            pltpu.VMEM((tf, D), BF16),
            pltpu.SemaphoreType.DMA,
            pltpu.SemaphoreType.DMA,
        ],
    )
    return pl.pallas_call(
        _moe_kernel,
        out_shape=jax.ShapeDtypeStruct((n_slots, D), F32),
        grid_spec=grid_spec,
        compiler_params=_cparams(("arbitrary", "arbitrary")),
        name="moe_ffn",
    )(sb_e, sb_blk0, sb_nblk, n_used, xs, w_gate, b_gate.reshape(L, E, 1, F), w_up, b_up.reshape(L, E, 1, F),
      w_down, b_down.reshape(L, E, 1, D))


def _combine_kernel(pos_ref, y_hbm, gates_ref, x_ref, mods_ref, o_ref, stage, sem):
    n = x_ref.shape[0]

    def row_copy(t, k):
        p = pos_ref[0, 0, t * TOP_K + k]
        return pltpu.make_async_copy(y_hbm.at[pl.ds(p, 1)], stage.at[k, pl.ds(t, 1)], sem)

    def issue(t, c):
        for k in range(TOP_K):
            row_copy(t, k).start()
        return c

    def drain(t, c):
        for k in range(TOP_K):
            row_copy(t, k).wait()
        return c

    lax.fori_loop(0, n, issue, 0)
    lax.fori_loop(0, n, drain, 0)
    gates = gates_ref[...]
    acc = gates[:, 0:1] * stage[0]
    for k in range(1, TOP_K):
        acc = acc + gates[:, k:k + 1] * stage[k]
    o_ref[...] = x_ref[...] + mods_ref[0][5:6] * acc


def _combine(y_slots, pos, gates, x, mods, n_ctx_rows, dec_seq):
    T, D = x.shape
    cb = 128
    return pl.pallas_call(
        _combine_kernel,
        out_shape=jax.ShapeDtypeStruct((T, D), F32),
        grid=(T // cb,),
        in_specs=[
            pl.BlockSpec((1, 1, cb * TOP_K), lambda i: (i, 0, 0), memory_space=pltpu.SMEM),
            pl.BlockSpec(memory_space=pl.ANY),
            pl.BlockSpec((cb, LANES), lambda i: (i, 0)),
            pl.BlockSpec((cb, D), lambda i: (i, 0)),
            pl.BlockSpec((1, N_MOD, D), lambda i: (_cond_index(i, cb, n_ctx_rows, dec_seq), 0, 0)),
        ],
        out_specs=pl.BlockSpec((cb, D), lambda i: (i, 0)),
        scratch_shapes=[pltpu.VMEM((TOP_K, cb, D), F32), pltpu.SemaphoreType.DMA],
        compiler_params=_cparams(("arbitrary",)),
        name="moe_combine",
    )(pos.reshape(T // cb, 1, cb * TOP_K), y_slots, gates, x, mods)


def _final_norm_kernel(x_ref, g_ref, o_ref):
    o_ref[...] = _rms_lanes(x_ref[...], g_ref[...])


def _final_norm(x, g):
    T, D = x.shape
    tm = 512
    return pl.pallas_call(
        _final_norm_kernel,
        out_shape=jax.ShapeDtypeStruct((T, D), F32),
        grid=(T // tm,),
        in_specs=[pl.BlockSpec((tm, D), lambda m: (m, 0)), pl.BlockSpec((1, D), lambda m: (0, 0))],
        out_specs=pl.BlockSpec((tm, D), lambda m: (m, 0)),
        compiler_params=_cparams(("arbitrary",)),
        name="final_norm",
    )(x, g)


def _rope_tables(n, head_dim):
    quarter = head_dim // 4
    inv = ROPE_THETA ** (-jnp.arange(quarter, dtype=F32) / quarter)
    pos = jnp.arange(n)
    row = (pos // GRID_W).astype(F32)[:, None] * inv
    col = (pos % GRID_W).astype(F32)[:, None] * inv
    cos = jnp.concatenate([jnp.cos(row), jnp.cos(row), jnp.cos(col), jnp.cos(col)], axis=1)
    sin = jnp.concatenate([-jnp.sin(row), jnp.sin(row), -jnp.sin(col), jnp.sin(col)], axis=1)
    reps = LANES // head_dim
    return jnp.tile(cos, (1, reps)), jnp.tile(sin, (1, reps))


def kernel(x_prompt, x_sample, c, cache_k_a, cache_v_a, cache_k_c, cache_v_c, c_ctx, w_mod, b_mod, g_norm1, g_norm2, w_in, lam_q1, lam_k1, lam_q2, lam_k2, g_subln_a, g_q_c, g_k_c, g_v_b, w_s_b, b_s_b, w_d, s_d, w_out, w_router, b_router, w_gate, b_gate, w_up, b_up, w_down, b_down, g_final):
    B, S, D = x_prompt.shape
    DB, DS, _ = x_sample.shape
    L = w_mod.shape[0]
    P = cache_k_a.shape[2]
    E = w_router.shape[2]
    n_ctx = B * S
    T = n_ctx + DB * DS
    GW = w_out.shape[1] // 4
    dk_a = cache_k_a.shape[-1]
    hd_c = cache_k_c.shape[-1]

    x = jnp.concatenate([x_prompt.reshape(n_ctx, D), x_sample.reshape(DB * DS, D)], axis=0)
    cond8 = jnp.zeros((8, D), F32).at[0].set(c_ctx).at[1:1 + DB].set(c)
    mod = _modulation(cond8, w_mod, b_mod).reshape(L, 8, N_MOD, D)

    cos_a, sin_a = _rope_tables(DS, dk_a)
    cos_c, sin_c = _rope_tables(DS, hd_c)
    caches_all = (cache_k_a.reshape(DB, L, P, GW), cache_v_a.reshape(DB, L, P, GW),
                  cache_k_c.reshape(DB, L, P, GW // 2), cache_v_c.reshape(DB, L, P, GW // 2))

    n_slots_max = -(-(T * TOP_K + E * (MOE_ROWS - 1)) // MOE_ROWS)
    n_sb = E + n_slots_max // SB_BLOCKS

    states = []
    for l in range(L):
        mods = mod[l]
        lam0 = 0.8 - 0.6 * math.exp(-0.3 * l)
        proj = _inproj(x, g_norm1[l][None, :], mods, w_in, l, n_ctx, DS)
        params = (jnp.stack([lam_q1[l], lam_k1[l], lam_q2[l], lam_k2[l]]), g_subln_a[l][None, :],
                  g_q_c[l][None, :], g_k_c[l][None, :], g_v_b[l][None, :], w_s_b[l], b_s_b[l].T,
                  w_d[l], s_d[l][None, :])
        mix_c, kc_norm = _mixer(proj, l, 0, B, S, lam0, params)
        mix_s = _mixer(proj, l, n_ctx, DB, DS, lam0, params, caches=caches_all,
                       rope_tabs=(cos_a, sin_a, cos_c, sin_c))
        pc = proj[:n_ctx]
        states.append((pc[:, GW:2 * GW], pc[:, 2 * GW:3 * GW], kc_norm, pc[:, 6 * GW + GW // 2:7 * GW]))
        x = _outproj(mix_c, mix_s, w_out, x, mods, l, n_ctx, DS)
        h, top_idx, gates = _router(x, g_norm2[l][None, :], mods, w_router[l], b_router[l], n_ctx, DS)
        pos, slot_tok, plan = _route_plan(top_idx[:, :TOP_K], E, n_sb)
        xs = _gather_rows(h, slot_tok)
        y_slots = _moe_ffn(xs, plan, w_gate, b_gate, w_up, b_up, w_down, b_down, l, n_sb)
        x = _combine(y_slots, pos, gates, x, mods, n_ctx, DS)

    y = _final_norm(x, g_final[None, :])
    y_prompt = y[:n_ctx].reshape(B, S, D)
    y_sample = y[n_ctx:].reshape(DB, DS, D)
    h_a = GW // (2 * dk_a)
    new_k_a = jnp.stack([st[0].reshape(B, S, h_a, 2, dk_a) for st in states], axis=1)
    new_v_a = jnp.stack([st[1].reshape(B, S, h_a, 2 * dk_a) for st in states], axis=1)
    new_k_c = jnp.stack([st[2].reshape(B, S, GW // 2 // hd_c, hd_c) for st in states], axis=1)
    new_v_c = jnp.stack([st[3].reshape(B, S, GW // 2 // hd_c, hd_c) for st in states], axis=1)
    return (y_prompt, y_sample, new_k_a, new_v_a, new_k_c, new_v_c)
```

```python
import functools
import math

import jax
import jax.numpy as jnp
from jax import lax
from jax.experimental import pallas as pl
from jax.experimental.pallas import tpu as pltpu

F32 = jnp.float32
BF16 = jnp.bfloat16

EPS = 1e-6
GRID_W = 64
ROPE_THETA = 10000.0
N_MOD = 6
TOP_K = 4
SWIGLU_LIMIT = 7.0
SWIGLU_ALPHA = 1.702
POOL_WINDOWS = (2, 4, 8, 16)

LANES = 128
QB = 128
MOE_ROWS = 128
SB_BLOCKS = 16
FFN_CHUNK_LOG2 = 2
FFN_CHUNK = 1 << FFN_CHUNK_LOG2
VMEM_LIMIT = 56 * 1024 * 1024


def _cparams(sem, vmem=VMEM_LIMIT):
    return pltpu.CompilerParams(dimension_semantics=sem, vmem_limit_bytes=vmem)


def _dot(a, b):
    return jnp.dot(a, b, preferred_element_type=F32)


def _dot_nt(a, b):
    return lax.dot_general(a, b, (((1,), (1,)), ((), ())), preferred_element_type=F32)


def _rms_lanes(x, g):
    ms = jnp.mean(x * x, axis=-1, keepdims=True)
    return x * lax.rsqrt(ms + EPS) * g


def _mod_kernel(cond_ref, w_ref, b_ref, o_ref):
    c = cond_ref[...]
    a = (c * jax.nn.sigmoid(c)).astype(BF16)
    o_ref[0] = _dot(a, w_ref[0].astype(BF16)) + b_ref[0]


def _modulation(cond8, w_mod, b_mod):
    L, D, NM = w_mod.shape
    tn = 1024
    return pl.pallas_call(
        _mod_kernel,
        out_shape=jax.ShapeDtypeStruct((L, 8, NM), F32),
        grid=(L, NM // tn),
        in_specs=[
            pl.BlockSpec((8, D), lambda l, n: (0, 0)),
            pl.BlockSpec((1, D, tn), lambda l, n: (l, 0, n)),
            pl.BlockSpec((1, 1, tn), lambda l, n: (l, 0, n)),
        ],
        out_specs=pl.BlockSpec((1, 8, tn), lambda l, n: (l, 0, n)),
        compiler_params=_cparams(("arbitrary", "arbitrary")),
        name="modulation",
    )(cond8, w_mod, b_mod.reshape(L, 1, NM))


def _cond_index(row_block, rows_per_block, n_ctx_rows, dec_seq):
    r0 = row_block * rows_per_block
    return jnp.where(r0 < n_ctx_rows, 0, 1 + (r0 - n_ctx_rows) // dec_seq)


def _inproj_kernel(x_ref, g_ref, mods_ref, w_ref, o_ref, h_ref, *, shift_row, scale_row, chunk):
    @pl.when(pl.program_id(1) == 0)
    def _():
        m = mods_ref[0]
        shift = m[shift_row:shift_row + 1]
        scale1 = 1.0 + m[scale_row:scale_row + 1]
        g = g_ref[...]

        def body(i, carry):
            r = pl.multiple_of(i * chunk, chunk)
            x = x_ref[pl.ds(r, chunk), :]
            h_ref[pl.ds(r, chunk), :] = (_rms_lanes(x, g) * scale1 + shift).astype(BF16)
            return carry

        lax.fori_loop(0, x_ref.shape[0] // chunk, body, 0)

    o_ref[...] = _dot(h_ref[...], w_ref[0].astype(BF16))


def _inproj(x, g, mods, w_in, l, n_ctx_rows, dec_seq):
    T, D = x.shape
    PW = w_in.shape[2]
    tm, tn = 1024, 512
    kern = functools.partial(_inproj_kernel, shift_row=0, scale_row=1, chunk=128)
    return pl.pallas_call(
        kern,
        out_shape=jax.ShapeDtypeStruct((T, PW), F32),
        grid=(T // tm, PW // tn),
        in_specs=[
            pl.BlockSpec((tm, D), lambda m, n: (m, 0)),
            pl.BlockSpec((1, D), lambda m, n: (0, 0)),
            pl.BlockSpec((1, N_MOD, D), lambda m, n: (_cond_index(m, tm, n_ctx_rows, dec_seq), 0, 0)),
            pl.BlockSpec((1, D, tn), lambda m, n: (l, 0, n)),
        ],
        out_specs=pl.BlockSpec((tm, tn), lambda m, n: (m, n)),
        scratch_shapes=[pltpu.VMEM((tm, D), BF16)],
        compiler_params=_cparams(("arbitrary", "arbitrary")),
        name="inproj",
    )(x, g, mods, w_in)


def _rope(x, cos, sin, q):
    lane = lax.broadcasted_iota(jnp.int32, x.shape, 1)
    first_half = (lane & q) == 0
    partner = jnp.where(first_half, pltpu.roll(x, LANES - q, 1), pltpu.roll(x, q, 1))
    return x * cos + partner * sin


def _softmax_parts(s):
    m = jnp.max(s, axis=-1, keepdims=True)
    e = jnp.exp(s - m)
    return e, jnp.sum(e, axis=-1, keepdims=True)


def _mixer_kernel(*refs, seq, n_cache, lam0, rope):
    it = iter(refs)
    qa_ref, ub_ref, vb_ref, qc_ref = next(it), next(it), next(it), next(it)
    ka_ref, va_ref, kv_ref, pd_ref = next(it), next(it), next(it), next(it)
    if n_cache:
        cka_ref, cva_ref, ckc_ref, cvc_ref = next(it), next(it), next(it), next(it)
    if rope:
        cos_a_ref, sin_a_ref, cos_c_ref, sin_c_ref = next(it), next(it), next(it), next(it)
    lamp_ref, gsub_ref, gq_ref, gk_ref, gvb_ref = next(it), next(it), next(it), next(it), next(it)
    ws_ref, bst_ref, wd_ref, sd_ref = next(it), next(it), next(it), next(it)
    mix_ref = next(it)
    kcn_ref = None if n_cache else next(it)
    ka_s, va_s, kc_s, vc_s = next(it), next(it), next(it), next(it)

    j = pl.program_id(1)
    row0 = pl.multiple_of(j * QB, QB)
    n_heads = qa_ref.shape[1] // LANES
    n_kv = kc_s.shape[1] // LANES
    dk_a = LANES // 2

    @pl.when(j == 0)
    def _():
        prep = 256
        for c in range(seq // prep):
            rows = slice(c * prep, (c + 1) * prep)
            for b in range(n_heads):
                cols = slice(b * LANES, (b + 1) * LANES)
                k = ka_ref[rows, cols]
                if rope:
                    k = _rope(k, cos_a_ref[rows, :], sin_a_ref[rows, :], dk_a // 4)
                ka_s[rows, cols] = k.astype(BF16)
                va_s[rows, cols] = va_ref[rows, cols].astype(BF16)
            for b in range(n_kv):
                cols = slice(b * LANES, (b + 1) * LANES)
                k = _rms_lanes(kv_ref[rows, cols], gk_ref[...])
                if kcn_ref is not None:
                    kcn_ref[rows, cols] = k
                if rope:
                    k = _rope(k, cos_c_ref[rows, :], sin_c_ref[rows, :], LANES // 4)
                kc_s[rows, cols] = k.astype(BF16)
                vcols = slice((n_kv + b) * LANES, (n_kv + b + 1) * LANES)
                vc_s[rows, cols] = kv_ref[rows, vcols].astype(BF16)
        if n_cache:
            tail = slice(seq, seq + n_cache)
            ka_s[tail, :] = cka_ref[0].astype(BF16)
            va_s[tail, :] = cva_ref[0].astype(BF16)
            kc_s[tail, :] = ckc_ref[0].astype(BF16)
            vc_s[tail, :] = cvc_ref[0].astype(BF16)

    lane = lax.broadcasted_iota(jnp.int32, (QB, LANES), 1)

    lp = lamp_ref[...]
    lam = (jnp.exp(jnp.sum(lp[0:1] * lp[1:2], axis=-1, keepdims=True))
           - jnp.exp(jnp.sum(lp[2:3] * lp[3:4], axis=-1, keepdims=True)) + lam0)
    scale_a = dk_a ** -0.5
    if rope:
        cos_a = cos_a_ref[pl.ds(row0, QB), :]
        sin_a = sin_a_ref[pl.ds(row0, QB), :]
        cos_c = cos_c_ref[pl.ds(row0, QB), :]
        sin_c = sin_c_ref[pl.ds(row0, QB), :]
    for h in range(n_heads):
        cols = slice(h * LANES, (h + 1) * LANES)
        q = qa_ref[:, cols]
        if rope:
            q = _rope(q, cos_a, sin_a, dk_a // 4)
        k = ka_s[:, cols]
        q0 = jnp.where(lane < dk_a, q, 0.0).astype(BF16)
        q1 = jnp.where(lane >= dk_a, q, 0.0).astype(BF16)
        e0, l0 = _softmax_parts(_dot_nt(q0, k) * scale_a)
        e1, l1 = _softmax_parts(_dot_nt(q1, k) * scale_a)
        a = e0 * (1.0 / l0) - e1 * (lam / l1)
        o = _dot(a.astype(BF16), va_s[:, cols])
        mix_ref[:, cols] = (_rms_lanes(o, gsub_ref[...]) * (1.0 - lam0)).astype(BF16)

    off_b = n_heads * LANES
    for g in range(ub_ref.shape[1] // LANES):
        cols = slice(g * LANES, (g + 1) * LANES)
        vn = _rms_lanes(vb_ref[:, cols], gvb_ref[:, cols])
        mixed = _dot(ws_ref[g].astype(BF16), vn.astype(BF16)) + bst_ref[:, g:g + 1]
        mix_ref[:, off_b + g * LANES: off_b + (g + 1) * LANES] = (ub_ref[:, cols] * mixed).astype(BF16)

    off_c = off_b + ub_ref.shape[1]
    scale_c = LANES ** -0.5
    for h in range(n_heads):
        cols = slice(h * LANES, (h + 1) * LANES)
        q = _rms_lanes(qc_ref[:, cols], gq_ref[...])
        if rope:
            q = _rope(q, cos_c, sin_c, LANES // 4)
        kv = h // (n_heads // n_kv)
        kcols = slice(kv * LANES, (kv + 1) * LANES)
        e, l = _softmax_parts(_dot_nt(q.astype(BF16), kc_s[:, kcols]) * scale_c)
        p = e * (1.0 / l)
        o = _dot(p.astype(BF16), vc_s[:, kcols])
        mix_ref[:, off_c + h * LANES: off_c + (h + 1) * LANES] = o.astype(BF16)

    off_d = off_c + qc_ref.shape[1]
    win = min(3 * QB, seq)
    start = pl.multiple_of(jnp.clip(row0 - QB, 0, seq - win), QB)
    t = row0 + lax.broadcasted_iota(jnp.int32, (QB, win), 0)
    col = start + lax.broadcasted_iota(jnp.int32, (QB, win), 1)
    t1 = row0 + lax.broadcasted_iota(jnp.int32, (QB, 1), 0)
    for g, w in enumerate(POOL_WINDOWS):
        cols = slice(g * LANES, (g + 1) * LANES)
        left, right = w // 2, w - 1 - w // 2
        band = jnp.where((col >= t - left) & (col <= t + right), 1.0, 0.0).astype(BF16)
        cnt = (jnp.minimum(t1 + right + 1, seq) - jnp.maximum(t1 - left, 0)).astype(F32)
        xw = pd_ref[pl.ds(start, win), cols]
        hi = xw.astype(BF16)
        lo = (xw - hi.astype(F32)).astype(BF16)
        wsum = _dot(band, hi) + _dot(band, lo)
        xg = pd_ref[pl.ds(row0, QB), cols]
        pooled = wsum / cnt - xg
        y = _dot(pooled.astype(BF16), wd_ref[g].astype(BF16)) * sd_ref[:, cols]
        mix_ref[:, off_d + g * LANES: off_d + (g + 1) * LANES] = y.astype(BF16)


def _mixer(proj, l, row_start, n_seq, seq, lam0, params, caches=None, rope_tabs=None):
    T, PW = proj.shape
    GW = PW // 8
    n_qb = seq // QB
    qb0 = row_start // QB
    sb0 = row_start // seq
    n_cache = 0 if caches is None else caches[0].shape[2]
    n_keys = seq + n_cache

    def qspec(col):
        return pl.BlockSpec((QB, GW), lambda s, j: (qb0 + s * n_qb + j, col))

    def sspec(col):
        return pl.BlockSpec((seq, GW), lambda s, j: (sb0 + s, col))

    def full(shape):
        nd = len(shape)
        return pl.BlockSpec(shape, lambda s, j: (0,) * nd)

    in_specs = [qspec(0), qspec(3), qspec(4), qspec(5), sspec(1), sspec(2), sspec(6), sspec(7)]
    args = [proj] * 8
    if caches is not None:
        for c in caches:
            in_specs.append(pl.BlockSpec((None, 1, c.shape[2], c.shape[3]), lambda s, j: (s, l, 0, 0)))
            args.append(c)
    if rope_tabs is not None:
        for tab in rope_tabs:
            in_specs.append(full(tab.shape))
            args.append(tab)
    for p in params:
        in_specs.append(full(p.shape))
        args.append(p)

    MW = 4 * GW
    mix_spec = pl.BlockSpec((QB, MW), lambda s, j: (s * n_qb + j, 0))
    if caches is None:
        out_shape = (jax.ShapeDtypeStruct((n_seq * seq, MW), BF16),
                     jax.ShapeDtypeStruct((n_seq * seq, GW // 2), F32))
        out_specs = (mix_spec, pl.BlockSpec((seq, GW // 2), lambda s, j: (s, 0)))
    else:
        out_shape = jax.ShapeDtypeStruct((n_seq * seq, MW), BF16)
        out_specs = mix_spec

    kern = functools.partial(_mixer_kernel, seq=seq, n_cache=n_cache, lam0=lam0,
                             rope=rope_tabs is not None)
    return pl.pallas_call(
        kern,
        out_shape=out_shape,
        grid=(n_seq, n_qb),
        in_specs=in_specs,
        out_specs=out_specs,
        scratch_shapes=[pltpu.VMEM((n_keys, GW), BF16), pltpu.VMEM((n_keys, GW), BF16),
                        pltpu.VMEM((n_keys, GW // 2), BF16), pltpu.VMEM((n_keys, GW // 2), BF16)],
        compiler_params=_cparams(("arbitrary", "arbitrary")),
        name="mixer_latent" if caches is not None else "mixer_context",
    )(*args)


def _outproj_kernel(mix_c_ref, mix_s_ref, w_ref, x_ref, mods_ref, o_ref, *, gate_row, n_ctx_tiles):
    tn = o_ref.shape[1]
    gcols = pl.ds(pl.multiple_of(pl.program_id(1) * tn, tn), tn)
    gate = mods_ref[0, gate_row:gate_row + 1, gcols]
    w = w_ref[0].astype(BF16)
    is_ctx = pl.program_id(0) < n_ctx_tiles

    @pl.when(is_ctx)
    def _():
        o_ref[...] = x_ref[...] + gate * _dot(mix_c_ref[...], w)

    @pl.when(jnp.logical_not(is_ctx))
    def _():
        o_ref[...] = x_ref[...] + gate * _dot(mix_s_ref[...], w)


def _outproj(mix_c, mix_s, w_out, x, mods, l, n_ctx_rows, dec_seq):
    T, D = x.shape
    MW = mix_c.shape[1]
    tm, tn = 1024, 512
    nct = n_ctx_rows // tm
    kern = functools.partial(_outproj_kernel, gate_row=2, n_ctx_tiles=nct)
    return pl.pallas_call(
        kern,
        out_shape=jax.ShapeDtypeStruct((T, D), F32),
        grid=(T // tm, D // tn),
        in_specs=[
            pl.BlockSpec((tm, MW), lambda m, n: (jnp.minimum(m, nct - 1), 0)),
            pl.BlockSpec((tm, MW), lambda m, n: (jnp.maximum(m - nct, 0), 0)),
            pl.BlockSpec((1, MW, tn), lambda m, n: (l, 0, n)),
            pl.BlockSpec((tm, tn), lambda m, n: (m, n)),
            pl.BlockSpec((1, N_MOD, D), lambda m, n: (_cond_index(m, tm, n_ctx_rows, dec_seq), 0, 0)),
        ],
        out_specs=pl.BlockSpec((tm, tn), lambda m, n: (m, n)),
        compiler_params=_cparams(("arbitrary", "arbitrary")),
        name="outproj",
    )(mix_c, mix_s, w_out, x, mods)


def _split_bf16(x):
    hi = x.astype(BF16)
    return hi, (x - hi.astype(F32)).astype(BF16)


def _router_kernel(x_ref, g_ref, mods_ref, wr_ref, br_ref, h_ref, idx_ref, gate_ref, *, n_experts):
    m = mods_ref[0]
    h = _rms_lanes(x_ref[...], g_ref[...]) * (1.0 + m[4:5]) + m[3:4]
    h_ref[...] = h
    h_hi, h_lo = _split_bf16(h)
    w_hi, w_lo = _split_bf16(wr_ref[...])
    logits = _dot(h_hi, w_hi) + _dot(h_hi, w_lo) + _dot(h_lo, w_hi) + br_ref[...]
    lane = lax.broadcasted_iota(jnp.int32, logits.shape, 1)
    lane_f = lane.astype(F32)
    neg = jnp.float32(-jnp.inf)
    logits = jnp.where(lane < n_experts, logits, neg)
    idx_out = jnp.zeros(logits.shape, F32)
    val_out = jnp.zeros(logits.shape, F32)
    top = None
    denom = None
    for k in range(TOP_K):
        v = jnp.max(logits, axis=-1, keepdims=True)
        i = jnp.min(jnp.where(logits == v, lane_f, float(LANES)), axis=-1, keepdims=True)
        if k == 0:
            top = v
        e = jnp.exp(v - top)
        denom = e if denom is None else denom + e
        idx_out = jnp.where(lane == k, i, idx_out)
        val_out = jnp.where(lane == k, e, val_out)
        logits = jnp.where(lane_f == i, neg, logits)
    idx_ref[...] = idx_out.astype(jnp.int32)
    gate_ref[...] = val_out / denom


def _router(x, g, mods, w_router_l, b_router_l, n_ctx_rows, dec_seq):
    T, D = x.shape
    E = w_router_l.shape[1]
    tm = 512
    wr = jnp.zeros((D, LANES), F32).at[:, :E].set(w_router_l)
    br = jnp.zeros((1, LANES), F32).at[0, :E].set(b_router_l)
    kern = functools.partial(_router_kernel, n_experts=E)
    return pl.pallas_call(
        kern,
        out_shape=(jax.ShapeDtypeStruct((T, D), F32),
                   jax.ShapeDtypeStruct((T, LANES), jnp.int32),
                   jax.ShapeDtypeStruct((T, LANES), F32)),
        grid=(T // tm,),
        in_specs=[
            pl.BlockSpec((tm, D), lambda m: (m, 0)),
            pl.BlockSpec((1, D), lambda m: (0, 0)),
            pl.BlockSpec((1, N_MOD, D), lambda m: (_cond_index(m, tm, n_ctx_rows, dec_seq), 0, 0)),
            pl.BlockSpec((D, LANES), lambda m: (0, 0)),
            pl.BlockSpec((1, LANES), lambda m: (0, 0)),
        ],
        out_specs=(pl.BlockSpec((tm, D), lambda m: (m, 0)),
                   pl.BlockSpec((tm, LANES), lambda m: (m, 0)),
                   pl.BlockSpec((tm, LANES), lambda m: (m, 0))),
        compiler_params=_cparams(("arbitrary",)),
        name="norm_router",
    )(x, g, mods, wr, br)


def _route_plan(top_idx, n_experts, n_sb):
    T, K = top_idx.shape
    flat_e = top_idx.reshape(-1)
    onehot = (flat_e[:, None] == jnp.arange(n_experts, dtype=jnp.int32)[None, :]).astype(jnp.int32)
    csum = jnp.cumsum(onehot, axis=0)
    counts = csum[-1]
    rank = jnp.take_along_axis(csum, flat_e[:, None], axis=1)[:, 0] - 1
    nblk = (counts + MOE_ROWS - 1) // MOE_ROWS
    blk_start = jnp.cumsum(nblk) - nblk
    pos = blk_start[flat_e] * MOE_ROWS + rank
    n_slots = -(-(T * K + n_experts * (MOE_ROWS - 1)) // MOE_ROWS) * MOE_ROWS
    tok = jnp.arange(T * K, dtype=jnp.int32) // K
    slot_tok = jnp.zeros((n_slots,), jnp.int32).at[pos].set(tok)
    nsb = (nblk + SB_BLOCKS - 1) // SB_BLOCKS
    sb_end = jnp.cumsum(nsb)
    s = jnp.arange(n_sb, dtype=jnp.int32)
    total = sb_end[-1]
    sc = jnp.minimum(s, total - 1)
    e = jnp.sum((sb_end[None, :] <= sc[:, None]).astype(jnp.int32), axis=1)
    k = sc - (sb_end[e] - nsb[e])
    sb_blk0 = (blk_start[e] + k * SB_BLOCKS).astype(jnp.int32)
    sb_nblk = jnp.where(s < total, jnp.minimum(SB_BLOCKS, nblk[e] - k * SB_BLOCKS), 0).astype(jnp.int32)
    n_used = jnp.sum(nblk).astype(jnp.int32).reshape(1)
    return pos.reshape(T, K).astype(jnp.int32), slot_tok, (e, sb_blk0, sb_nblk, n_used)


def _gather_kernel(tok_ref, h_hbm, o_ref, stage, sem):
    n = o_ref.shape[0]

    def row_copy(i):
        t = tok_ref[0, 0, i]
        return pltpu.make_async_copy(h_hbm.at[pl.ds(t, 1)], stage.at[pl.ds(i, 1)], sem)

    def issue(i, c):
        row_copy(i).start()
        return c

    lax.fori_loop(0, n, issue, 0, unroll=8)
    pltpu.make_async_copy(h_hbm.at[pl.ds(0, n)], stage, sem).wait()
    o_ref[...] = stage[...].astype(BF16)


def _gather_rows(h, slot_tok):
    T, D = h.shape
    n_slots = slot_tok.shape[0]
    gb = 256
    return pl.pallas_call(
        _gather_kernel,
        out_shape=jax.ShapeDtypeStruct((n_slots, D), BF16),
        grid=(n_slots // gb,),
        in_specs=[
            pl.BlockSpec((1, 1, gb), lambda i: (i, 0, 0), memory_space=pltpu.SMEM),
            pl.BlockSpec(memory_space=pl.ANY),
        ],
        out_specs=pl.BlockSpec((gb, D), lambda i: (i, 0)),
        scratch_shapes=[pltpu.VMEM((gb, D), F32), pltpu.SemaphoreType.DMA],
        compiler_params=_cparams(("arbitrary",)),
        name="moe_gather",
    )(slot_tok.reshape(n_slots // gb, 1, gb), h)


def _moe_kernel(sb_e_ref, sb_blk0_ref, sb_nblk_ref, n_used_ref, xs_hbm, wg_ref, bg_ref, wu_ref, bu_ref,
                wd_ref, bd_ref, y_hbm, x_s, y_s, wgu_s, wd_s, sem_in, sem_out):
    s = pl.program_id(0)
    f = pl.program_id(1)
    n_f = pl.num_programs(1)
    nblk = sb_nblk_ref[s]
    blk0 = sb_blk0_ref[s]
    tf = wg_ref.shape[3]

    def in_copy(r):
        src = xs_hbm.at[pl.ds(pl.multiple_of((blk0 + r) * MOE_ROWS, MOE_ROWS), MOE_ROWS)]
        return pltpu.make_async_copy(src, x_s.at[pl.ds(pl.multiple_of(r * MOE_ROWS, MOE_ROWS), MOE_ROWS)], sem_in)

    def out_copy(r):
        dst = y_hbm.at[pl.ds(pl.multiple_of((blk0 + r) * MOE_ROWS, MOE_ROWS), MOE_ROWS)]
        return pltpu.make_async_copy(y_s.at[pl.ds(pl.multiple_of(r * MOE_ROWS, MOE_ROWS), MOE_ROWS)], dst, sem_out)

    def each_block(fn):
        def body(r, c):
            fn(r)
            return c
        lax.fori_loop(0, nblk, body, 0)

    @pl.when(nblk > 0)
    def _():
        @pl.when(f == 0)
        def _():
            each_block(lambda r: in_copy(r).start())

        wgu_s[:, :tf] = wg_ref[0, 0].astype(BF16)
        wgu_s[:, tf:] = wu_ref[0, 0].astype(BF16)
        wd_s[...] = wd_ref[0, 0].astype(BF16)
        bg = bg_ref[0, 0]
        bu = bu_ref[0, 0]
        bd = bd_ref[0, 0]

        @pl.when(f == 0)
        def _():
            each_block(lambda r: in_copy(r).wait())
            bias_rows = jnp.broadcast_to(bd, (MOE_ROWS, bd.shape[1]))

            def init(r):
                y_s[pl.ds(pl.multiple_of(r * MOE_ROWS, MOE_ROWS), MOE_ROWS), :] = bias_rows

            each_block(init)

        def ffn(blk, nb):
            rows = pl.ds(pl.multiple_of(blk * MOE_ROWS, MOE_ROWS), nb * MOE_ROWS)
            gu = _dot(x_s[rows, :], wgu_s[...])
            g = jnp.minimum(gu[:, :tf] + bg, SWIGLU_LIMIT)
            u = jnp.clip(gu[:, tf:] + bu, -SWIGLU_LIMIT, SWIGLU_LIMIT)
            act = (u + 1.0) * g * jax.nn.sigmoid(SWIGLU_ALPHA * g)
            y_s[rows, :] += _dot(act.astype(BF16), wd_s[...])

        n_big = lax.shift_right_logical(nblk, FFN_CHUNK_LOG2)

        def big(c, carry):
            ffn(c * FFN_CHUNK, FFN_CHUNK)
            return carry

        lax.fori_loop(0, n_big, big, 0)
        done = n_big * FFN_CHUNK
        nb = FFN_CHUNK // 2
        while nb >= 1:
            @pl.when((nblk & nb) != 0)
            def _(done=done, nb=nb):
                ffn(done, nb)
            done = done + (nblk & nb)
            nb //= 2

        @pl.when(f == n_f - 1)
        def _():
            each_block(lambda r: out_copy(r).start())
            each_block(lambda r: out_copy(r).wait())

    @pl.when(jnp.logical_and(s == pl.num_programs(0) - 1, f == n_f - 1))
    def _():
        y_s[0:MOE_ROWS, :] = jnp.zeros((MOE_ROWS, y_s.shape[1]), F32)

        def tail_copy(r):
            dst = y_hbm.at[pl.ds(pl.multiple_of(r * MOE_ROWS, MOE_ROWS), MOE_ROWS)]
            return pltpu.make_async_copy(y_s.at[0:MOE_ROWS], dst, sem_out)

        def start(r, c):
            tail_copy(r).start()
            return c

        def wait(r, c):
            tail_copy(r).wait()
            return c

        lax.fori_loop(n_used_ref[0], y_hbm.shape[0] // MOE_ROWS, start, 0)
        lax.fori_loop(n_used_ref[0], y_hbm.shape[0] // MOE_ROWS, wait, 0)


def _moe_ffn(xs, plan, w_gate, b_gate, w_up, b_up, w_down, b_down, l, n_sb):
    sb_e, sb_blk0, sb_nblk, n_used = plan
    n_slots, D = xs.shape
    L, E, _, F = w_gate.shape
    tf = 256
    rows = SB_BLOCKS * MOE_ROWS
    n_f = F // tf

    def ftile(s, f, n):
        return jnp.where(n[s] > 0, f, n_f - 1)

    grid_spec = pltpu.PrefetchScalarGridSpec(
        num_scalar_prefetch=4,
        grid=(n_sb, n_f),
        in_specs=[
            pl.BlockSpec(memory_space=pl.ANY),
            pl.BlockSpec((1, 1, D, tf), lambda s, f, e, b, n, u: (l, e[s], 0, ftile(s, f, n))),
            pl.BlockSpec((1, 1, 1, tf), lambda s, f, e, b, n, u: (l, e[s], 0, ftile(s, f, n))),
            pl.BlockSpec((1, 1, D, tf), lambda s, f, e, b, n, u: (l, e[s], 0, ftile(s, f, n))),
            pl.BlockSpec((1, 1, 1, tf), lambda s, f, e, b, n, u: (l, e[s], 0, ftile(s, f, n))),
            pl.BlockSpec((1, 1, tf, D), lambda s, f, e, b, n, u: (l, e[s], ftile(s, f, n), 0)),
            pl.BlockSpec((1, 1, 1, D), lambda s, f, e, b, n, u: (l, e[s], 0, 0)),
        ],
        out_specs=pl.BlockSpec(memory_space=pl.ANY),
        scratch_shapes=[
            pltpu.VMEM((rows, D), BF16),
            pltpu.VMEM((rows, D), F32),
            pltpu.VMEM((D, 2 * tf), BF16),
            pltpu.VMEM((tf, D), BF16),
            pltpu.SemaphoreType.DMA,
            pltpu.SemaphoreType.DMA,
        ],
    )
    return pl.pallas_call(
        _moe_kernel,
        out_shape=jax.ShapeDtypeStruct((n_slots, D), F32),
        grid_spec=grid_spec,
        compiler_params=_cparams(("arbitrary", "arbitrary")),
        name="moe_ffn",
    )(sb_e, sb_blk0, sb_nblk, n_used, xs, w_gate, b_gate.reshape(L, E, 1, F), w_up, b_up.reshape(L, E, 1, F),
      w_down, b_down.reshape(L, E, 1, D))


def _combine_kernel(pos_ref, y_hbm, gates_ref, x_ref, mods_ref, o_ref, stage, sem):
    n = x_ref.shape[0]

    def row_copy(t, k):
        p = pos_ref[0, 0, t * TOP_K + k]
        return pltpu.make_async_copy(y_hbm.at[pl.ds(p, 1)], stage.at[k, pl.ds(t, 1)], sem)

    def issue(t, c):
        for k in range(TOP_K):
            row_copy(t, k).start()
        return c

    lax.fori_loop(0, n, issue, 0, unroll=2)
    for k in range(TOP_K):
        pltpu.make_async_copy(y_hbm.at[pl.ds(0, n)], stage.at[k], sem).wait()
    gates = gates_ref[...]
    acc = gates[:, 0:1] * stage[0]
    for k in range(1, TOP_K):
        acc = acc + gates[:, k:k + 1] * stage[k]
    o_ref[...] = x_ref[...] + mods_ref[0][5:6] * acc


def _combine(y_slots, pos, gates, x, mods, n_ctx_rows, dec_seq):
    T, D = x.shape
    cb = 128
    return pl.pallas_call(
        _combine_kernel,
        out_shape=jax.ShapeDtypeStruct((T, D), F32),
        grid=(T // cb,),
        in_specs=[
            pl.BlockSpec((1, 1, cb * TOP_K), lambda i: (i, 0, 0), memory_space=pltpu.SMEM),
            pl.BlockSpec(memory_space=pl.ANY),
            pl.BlockSpec((cb, LANES), lambda i: (i, 0)),
            pl.BlockSpec((cb, D), lambda i: (i, 0)),
            pl.BlockSpec((1, N_MOD, D), lambda i: (_cond_index(i, cb, n_ctx_rows, dec_seq), 0, 0)),
        ],
        out_specs=pl.BlockSpec((cb, D), lambda i: (i, 0)),
        scratch_shapes=[pltpu.VMEM((TOP_K, cb, D), F32), pltpu.SemaphoreType.DMA],
        compiler_params=_cparams(("arbitrary",)),
        name="moe_combine",
    )(pos.reshape(T // cb, 1, cb * TOP_K), y_slots, gates, x, mods)


def _final_norm_kernel(x_ref, g_ref, o_ref):
    o_ref[...] = _rms_lanes(x_ref[...], g_ref[...])


def _final_norm(x, g):
    T, D = x.shape
    tm = 512
    return pl.pallas_call(
        _final_norm_kernel,
        out_shape=jax.ShapeDtypeStruct((T, D), F32),
        grid=(T // tm,),
        in_specs=[pl.BlockSpec((tm, D), lambda m: (m, 0)), pl.BlockSpec((1, D), lambda m: (0, 0))],
        out_specs=pl.BlockSpec((tm, D), lambda m: (m, 0)),
        compiler_params=_cparams(("arbitrary",)),
        name="final_norm",
    )(x, g)


def _rope_tables(n, head_dim):
    quarter = head_dim // 4
    inv = ROPE_THETA ** (-jnp.arange(quarter, dtype=F32) / quarter)
    pos = jnp.arange(n)
    row = (pos // GRID_W).astype(F32)[:, None] * inv
    col = (pos % GRID_W).astype(F32)[:, None] * inv
    cos = jnp.concatenate([jnp.cos(row), jnp.cos(row), jnp.cos(col), jnp.cos(col)], axis=1)
    sin = jnp.concatenate([-jnp.sin(row), jnp.sin(row), -jnp.sin(col), jnp.sin(col)], axis=1)
    reps = LANES // head_dim
    return jnp.tile(cos, (1, reps)), jnp.tile(sin, (1, reps))


def kernel(x_prompt, x_sample, c, cache_k_a, cache_v_a, cache_k_c, cache_v_c, c_ctx, w_mod, b_mod, g_norm1, g_norm2, w_in, lam_q1, lam_k1, lam_q2, lam_k2, g_subln_a, g_q_c, g_k_c, g_v_b, w_s_b, b_s_b, w_d, s_d, w_out, w_router, b_router, w_gate, b_gate, w_up, b_up, w_down, b_down, g_final):
    B, S, D = x_prompt.shape
    DB, DS, _ = x_sample.shape
    L = w_mod.shape[0]
    P = cache_k_a.shape[2]
    E = w_router.shape[2]
    n_ctx = B * S
    T = n_ctx + DB * DS
    GW = w_out.shape[1] // 4
    dk_a = cache_k_a.shape[-1]
    hd_c = cache_k_c.shape[-1]

    x = jnp.concatenate([x_prompt.reshape(n_ctx, D), x_sample.reshape(DB * DS, D)], axis=0)
    cond8 = jnp.zeros((8, D), F32).at[0].set(c_ctx).at[1:1 + DB].set(c)
    mod = _modulation(cond8, w_mod, b_mod).reshape(L, 8, N_MOD, D)

    cos_a, sin_a = _rope_tables(DS, dk_a)
    cos_c, sin_c = _rope_tables(DS, hd_c)
    caches_all = (cache_k_a.reshape(DB, L, P, GW), cache_v_a.reshape(DB, L, P, GW),
                  cache_k_c.reshape(DB, L, P, GW // 2), cache_v_c.reshape(DB, L, P, GW // 2))

    n_slots_max = -(-(T * TOP_K + E * (MOE_ROWS - 1)) // MOE_ROWS)
    n_sb = E + n_slots_max // SB_BLOCKS

    states = []
    for l in range(L):
        mods = mod[l]
        lam0 = 0.8 - 0.6 * math.exp(-0.3 * l)
        proj = _inproj(x, g_norm1[l][None, :], mods, w_in, l, n_ctx, DS)
        params = (jnp.stack([lam_q1[l], lam_k1[l], lam_q2[l], lam_k2[l]]), g_subln_a[l][None, :],
                  g_q_c[l][None, :], g_k_c[l][None, :], g_v_b[l][None, :], w_s_b[l], b_s_b[l].T,
                  w_d[l], s_d[l][None, :])
        mix_c, kc_norm = _mixer(proj, l, 0, B, S, lam0, params)
        mix_s = _mixer(proj, l, n_ctx, DB, DS, lam0, params, caches=caches_all,
                       rope_tabs=(cos_a, sin_a, cos_c, sin_c))
        pc = proj[:n_ctx]
        states.append((pc[:, GW:2 * GW], pc[:, 2 * GW:3 * GW], kc_norm, pc[:, 6 * GW + GW // 2:7 * GW]))
        x = _outproj(mix_c, mix_s, w_out, x, mods, l, n_ctx, DS)
        h, top_idx, gates = _router(x, g_norm2[l][None, :], mods, w_router[l], b_router[l], n_ctx, DS)
        pos, slot_tok, plan = _route_plan(top_idx[:, :TOP_K], E, n_sb)
        xs = _gather_rows(h, slot_tok)
        y_slots = _moe_ffn(xs, plan, w_gate, b_gate, w_up, b_up, w_down, b_down, l, n_sb)
        x = _combine(y_slots, pos, gates, x, mods, n_ctx, DS)

    y = _final_norm(x, g_final[None, :])
    y_prompt = y[:n_ctx].reshape(B, S, D)
    y_sample = y[n_ctx:].reshape(DB, DS, D)
    h_a = GW // (2 * dk_a)
    new_k_a = jnp.stack([st[0].reshape(B, S, h_a, 2, dk_a) for st in states], axis=1)
    new_v_a = jnp.stack([st[1].reshape(B, S, h_a, 2 * dk_a) for st in states], axis=1)
    new_k_c = jnp.stack([st[2].reshape(B, S, GW // 2 // hd_c, hd_c) for st in states], axis=1)
    new_v_c = jnp.stack([st[3].reshape(B, S, GW // 2 // hd_c, hd_c) for st in states], axis=1)
    return (y_prompt, y_sample, new_k_a, new_v_a, new_k_c, new_v_c)
```

```python
import functools
import math

import jax
import jax.numpy as jnp
from jax import lax
from jax.experimental import pallas as pl
from jax.experimental.pallas import tpu as pltpu

F32 = jnp.float32
BF16 = jnp.bfloat16

EPS = 1e-6
GRID_W = 64
ROPE_THETA = 10000.0
N_MOD = 6
TOP_K = 4
SWIGLU_LIMIT = 7.0
SWIGLU_ALPHA = 1.702
POOL_WINDOWS = (2, 4, 8, 16)

LANES = 128
QB = 128
MOE_ROWS = 128
SB_BLOCKS = 16
XS_STAGES = 4
FFN_CHUNK_LOG2 = 2
FFN_CHUNK = 1 << FFN_CHUNK_LOG2
VMEM_LIMIT = 56 * 1024 * 1024


def _cparams(sem, vmem=VMEM_LIMIT):
    return pltpu.CompilerParams(dimension_semantics=sem, vmem_limit_bytes=vmem)


def _dot(a, b):
    return jnp.dot(a, b, preferred_element_type=F32)


def _dot_nt(a, b):
    return lax.dot_general(a, b, (((1,), (1,)), ((), ())), preferred_element_type=F32)


def _rms_lanes(x, g):
    ms = jnp.mean(x * x, axis=-1, keepdims=True)
    return x * lax.rsqrt(ms + EPS) * g


def _mod_kernel(cond_ref, w_ref, b_ref, o_ref):
    c = cond_ref[...]
    a = (c * jax.nn.sigmoid(c)).astype(BF16)
    o_ref[0] = _dot(a, w_ref[0].astype(BF16)) + b_ref[0]


def _modulation(cond8, w_mod, b_mod):
    L, D, NM = w_mod.shape
    tn = 1024
    return pl.pallas_call(
        _mod_kernel,
        out_shape=jax.ShapeDtypeStruct((L, 8, NM), F32),
        grid=(L, NM // tn),
        in_specs=[
            pl.BlockSpec((8, D), lambda l, n: (0, 0)),
            pl.BlockSpec((1, D, tn), lambda l, n: (l, 0, n)),
            pl.BlockSpec((1, 1, tn), lambda l, n: (l, 0, n)),
        ],
        out_specs=pl.BlockSpec((1, 8, tn), lambda l, n: (l, 0, n)),
        compiler_params=_cparams(("arbitrary", "arbitrary")),
        name="modulation",
    )(cond8, w_mod, b_mod.reshape(L, 1, NM))


def _cast_kernel(w_ref, o_ref):
    o_ref[...] = w_ref[...].astype(BF16)


def _to_bf16(w):
    L, K, N = w.shape
    tn = 512
    return pl.pallas_call(
        _cast_kernel,
        out_shape=jax.ShapeDtypeStruct(w.shape, BF16),
        grid=(L, N // tn),
        in_specs=[pl.BlockSpec((1, K, tn), lambda l, n: (l, 0, n))],
        out_specs=pl.BlockSpec((1, K, tn), lambda l, n: (l, 0, n)),
        compiler_params=_cparams(("arbitrary", "arbitrary")),
        name="weights_to_bf16",
    )(w)


def _cond_index(row_block, rows_per_block, n_ctx_rows, dec_seq):
    r0 = row_block * rows_per_block
    return jnp.where(r0 < n_ctx_rows, 0, 1 + (r0 - n_ctx_rows) // dec_seq)


def _inproj_kernel(x_ref, g_ref, mods_ref, w_ref, o_ref, h_ref, *, shift_row, scale_row, chunk):
    @pl.when(pl.program_id(1) == 0)
    def _():
        m = mods_ref[0]
        shift = m[shift_row:shift_row + 1]
        scale1 = 1.0 + m[scale_row:scale_row + 1]
        g = g_ref[...]

        def body(i, carry):
            r = pl.multiple_of(i * chunk, chunk)
            x = x_ref[pl.ds(r, chunk), :]
            h_ref[pl.ds(r, chunk), :] = (_rms_lanes(x, g) * scale1 + shift).astype(BF16)
            return carry

        lax.fori_loop(0, x_ref.shape[0] // chunk, body, 0)

    o_ref[...] = _dot(h_ref[...], w_ref[0])


def _inproj(x, g, mods, w_in, l, n_ctx_rows, dec_seq):
    T, D = x.shape
    PW = w_in.shape[2]
    tm, tn = 1024, 1024
    kern = functools.partial(_inproj_kernel, shift_row=0, scale_row=1, chunk=128)
    return pl.pallas_call(
        kern,
        out_shape=jax.ShapeDtypeStruct((T, PW), F32),
        grid=(T // tm, PW // tn),
        in_specs=[
            pl.BlockSpec((tm, D), lambda m, n: (m, 0)),
            pl.BlockSpec((1, D), lambda m, n: (0, 0)),
            pl.BlockSpec((1, N_MOD, D), lambda m, n: (_cond_index(m, tm, n_ctx_rows, dec_seq), 0, 0)),
            pl.BlockSpec((1, D, tn), lambda m, n: (l, 0, n)),
        ],
        out_specs=pl.BlockSpec((tm, tn), lambda m, n: (m, n)),
        scratch_shapes=[pltpu.VMEM((tm, D), BF16)],
        compiler_params=_cparams(("arbitrary", "arbitrary")),
        name="inproj",
    )(x, g, mods, w_in)


def _rope(x, cos, sin, q):
    lane = lax.broadcasted_iota(jnp.int32, x.shape, 1)
    first_half = (lane & q) == 0
    partner = jnp.where(first_half, pltpu.roll(x, LANES - q, 1), pltpu.roll(x, q, 1))
    return x * cos + partner * sin


def _softmax_parts(s):
    m = jnp.max(s, axis=-1, keepdims=True)
    e = jnp.exp(s - m)
    return e, jnp.sum(e, axis=-1, keepdims=True)


def _mixer_kernel(*refs, seq, n_cache, lam0, rope):
    it = iter(refs)
    qa_ref, ub_ref, vb_ref, qc_ref = next(it), next(it), next(it), next(it)
    ka_ref, va_ref, kv_ref, pd_ref = next(it), next(it), next(it), next(it)
    if n_cache:
        cka_ref, cva_ref, ckc_ref, cvc_ref = next(it), next(it), next(it), next(it)
    if rope:
        cos_a_ref, sin_a_ref, cos_c_ref, sin_c_ref = next(it), next(it), next(it), next(it)
    lamp_ref, gsub_ref, gq_ref, gk_ref, gvb_ref = next(it), next(it), next(it), next(it), next(it)
    ws_ref, bst_ref, wd_ref, sd_ref = next(it), next(it), next(it), next(it)
    mix_ref = next(it)
    kcn_ref = None if n_cache else next(it)
    ka_s, va_s, kc_s, vc_s = next(it), next(it), next(it), next(it)

    j = pl.program_id(1)
    row0 = pl.multiple_of(j * QB, QB)
    n_heads = qa_ref.shape[1] // LANES
    n_kv = kc_s.shape[1] // LANES
    dk_a = LANES // 2

    @pl.when(j == 0)
    def _():
        prep = 256
        for c in range(seq // prep):
            rows = slice(c * prep, (c + 1) * prep)
            for b in range(n_heads):
                cols = slice(b * LANES, (b + 1) * LANES)
                k = ka_ref[rows, cols]
                if rope:
                    k = _rope(k, cos_a_ref[rows, :], sin_a_ref[rows, :], dk_a // 4)
                ka_s[rows, cols] = k.astype(BF16)
                va_s[rows, cols] = va_ref[rows, cols].astype(BF16)
            for b in range(n_kv):
                cols = slice(b * LANES, (b + 1) * LANES)
                k = _rms_lanes(kv_ref[rows, cols], gk_ref[...])
                if kcn_ref is not None:
                    kcn_ref[rows, cols] = k
                if rope:
                    k = _rope(k, cos_c_ref[rows, :], sin_c_ref[rows, :], LANES // 4)
                kc_s[rows, cols] = k.astype(BF16)
                vcols = slice((n_kv + b) * LANES, (n_kv + b + 1) * LANES)
                vc_s[rows, cols] = kv_ref[rows, vcols].astype(BF16)
        if n_cache:
            tail = slice(seq, seq + n_cache)
            ka_s[tail, :] = cka_ref[0].astype(BF16)
            va_s[tail, :] = cva_ref[0].astype(BF16)
            kc_s[tail, :] = ckc_ref[0].astype(BF16)
            vc_s[tail, :] = cvc_ref[0].astype(BF16)

    lane = lax.broadcasted_iota(jnp.int32, (QB, LANES), 1)

    lp = lamp_ref[...]
    lam = (jnp.exp(jnp.sum(lp[0:1] * lp[1:2], axis=-1, keepdims=True))
           - jnp.exp(jnp.sum(lp[2:3] * lp[3:4], axis=-1, keepdims=True)) + lam0)
    scale_a = dk_a ** -0.5
    if rope:
        cos_a = cos_a_ref[pl.ds(row0, QB), :]
        sin_a = sin_a_ref[pl.ds(row0, QB), :]
        cos_c = cos_c_ref[pl.ds(row0, QB), :]
        sin_c = sin_c_ref[pl.ds(row0, QB), :]
    for h in range(n_heads):
        cols = slice(h * LANES, (h + 1) * LANES)
        q = qa_ref[:, cols]
        if rope:
            q = _rope(q, cos_a, sin_a, dk_a // 4)
        k = ka_s[:, cols]
        q0 = jnp.where(lane < dk_a, q, 0.0).astype(BF16)
        q1 = jnp.where(lane >= dk_a, q, 0.0).astype(BF16)
        e0, l0 = _softmax_parts(_dot_nt(q0, k) * scale_a)
        e1, l1 = _softmax_parts(_dot_nt(q1, k) * scale_a)
        a = e0 * (1.0 / l0) - e1 * (lam / l1)
        o = _dot(a.astype(BF16), va_s[:, cols])
        mix_ref[:, cols] = (_rms_lanes(o, gsub_ref[...]) * (1.0 - lam0)).astype(BF16)

    off_b = n_heads * LANES
    for g in range(ub_ref.shape[1] // LANES):
        cols = slice(g * LANES, (g + 1) * LANES)
        vn = _rms_lanes(vb_ref[:, cols], gvb_ref[:, cols])
        mixed = _dot(ws_ref[g].astype(BF16), vn.astype(BF16)) + bst_ref[:, g:g + 1]
        mix_ref[:, off_b + g * LANES: off_b + (g + 1) * LANES] = (ub_ref[:, cols] * mixed).astype(BF16)

    off_c = off_b + ub_ref.shape[1]
    scale_c = LANES ** -0.5
    for h in range(n_heads):
        cols = slice(h * LANES, (h + 1) * LANES)
        q = _rms_lanes(qc_ref[:, cols], gq_ref[...])
        if rope:
            q = _rope(q, cos_c, sin_c, LANES // 4)
        kv = h // (n_heads // n_kv)
        kcols = slice(kv * LANES, (kv + 1) * LANES)
        e, l = _softmax_parts(_dot_nt(q.astype(BF16), kc_s[:, kcols]) * scale_c)
        p = e * (1.0 / l)
        o = _dot(p.astype(BF16), vc_s[:, kcols])
        mix_ref[:, off_c + h * LANES: off_c + (h + 1) * LANES] = o.astype(BF16)

    off_d = off_c + qc_ref.shape[1]
    win = min(3 * QB, seq)
    start = pl.multiple_of(jnp.clip(row0 - QB, 0, seq - win), QB)
    t = row0 + lax.broadcasted_iota(jnp.int32, (QB, win), 0)
    col = start + lax.broadcasted_iota(jnp.int32, (QB, win), 1)
    t1 = row0 + lax.broadcasted_iota(jnp.int32, (QB, 1), 0)
    for g, w in enumerate(POOL_WINDOWS):
        cols = slice(g * LANES, (g + 1) * LANES)
        left, right = w // 2, w - 1 - w // 2
        band = jnp.where((col >= t - left) & (col <= t + right), 1.0, 0.0).astype(BF16)
        cnt = (jnp.minimum(t1 + right + 1, seq) - jnp.maximum(t1 - left, 0)).astype(F32)
        xw = pd_ref[pl.ds(start, win), cols]
        hi = xw.astype(BF16)
        lo = (xw - hi.astype(F32)).astype(BF16)
        wsum = _dot(band, hi) + _dot(band, lo)
        xg = pd_ref[pl.ds(row0, QB), cols]
        pooled = wsum / cnt - xg
        y = _dot(pooled.astype(BF16), wd_ref[g].astype(BF16)) * sd_ref[:, cols]
        mix_ref[:, off_d + g * LANES: off_d + (g + 1) * LANES] = y.astype(BF16)


def _mixer(proj, l, row_start, n_seq, seq, lam0, params, caches=None, rope_tabs=None):
    T, PW = proj.shape
    GW = PW // 8
    n_qb = seq // QB
    qb0 = row_start // QB
    sb0 = row_start // seq
    n_cache = 0 if caches is None else caches[0].shape[2]
    n_keys = seq + n_cache

    def qspec(col):
        return pl.BlockSpec((QB, GW), lambda s, j: (qb0 + s * n_qb + j, col))

    def sspec(col):
        return pl.BlockSpec((seq, GW), lambda s, j: (sb0 + s, col))

    def full(shape):
        nd = len(shape)
        return pl.BlockSpec(shape, lambda s, j: (0,) * nd)

    in_specs = [qspec(0), qspec(3), qspec(4), qspec(5), sspec(1), sspec(2), sspec(6), sspec(7)]
    args = [proj] * 8
    if caches is not None:
        for c in caches:
            in_specs.append(pl.BlockSpec((None, 1, c.shape[2], c.shape[3]), lambda s, j: (s, l, 0, 0)))
            args.append(c)
    if rope_tabs is not None:
        for tab in rope_tabs:
            in_specs.append(full(tab.shape))
            args.append(tab)
    for p in params:
        in_specs.append(full(p.shape))
        args.append(p)

    MW = 4 * GW
    mix_spec = pl.BlockSpec((QB, MW), lambda s, j: (s * n_qb + j, 0))
    if caches is None:
        out_shape = (jax.ShapeDtypeStruct((n_seq * seq, MW), BF16),
                     jax.ShapeDtypeStruct((n_seq * seq, GW // 2), F32))
        out_specs = (mix_spec, pl.BlockSpec((seq, GW // 2), lambda s, j: (s, 0)))
    else:
        out_shape = jax.ShapeDtypeStruct((n_seq * seq, MW), BF16)
        out_specs = mix_spec

    kern = functools.partial(_mixer_kernel, seq=seq, n_cache=n_cache, lam0=lam0,
                             rope=rope_tabs is not None)
    return pl.pallas_call(
        kern,
        out_shape=out_shape,
        grid=(n_seq, n_qb),
        in_specs=in_specs,
        out_specs=out_specs,
        scratch_shapes=[pltpu.VMEM((n_keys, GW), BF16), pltpu.VMEM((n_keys, GW), BF16),
                        pltpu.VMEM((n_keys, GW // 2), BF16), pltpu.VMEM((n_keys, GW // 2), BF16)],
        compiler_params=_cparams(("arbitrary", "arbitrary")),
        name="mixer_latent" if caches is not None else "mixer_context",
    )(*args)


def _outproj_kernel(mix_c_ref, mix_s_ref, w_ref, x_ref, mods_ref, o_ref, *, gate_row, n_ctx_tiles):
    tn = o_ref.shape[1]
    gcols = pl.ds(pl.multiple_of(pl.program_id(1) * tn, tn), tn)
    gate = mods_ref[0, gate_row:gate_row + 1, gcols]
    w = w_ref[0]
    is_ctx = pl.program_id(0) < n_ctx_tiles

    @pl.when(is_ctx)
    def _():
        o_ref[...] = x_ref[...] + gate * _dot(mix_c_ref[...], w)

    @pl.when(jnp.logical_not(is_ctx))
    def _():
        o_ref[...] = x_ref[...] + gate * _dot(mix_s_ref[...], w)


def _outproj(mix_c, mix_s, w_out, x, mods, l, n_ctx_rows, dec_seq):
    T, D = x.shape
    MW = mix_c.shape[1]
    tm, tn = 1024, 1024
    nct = n_ctx_rows // tm
    kern = functools.partial(_outproj_kernel, gate_row=2, n_ctx_tiles=nct)
    return pl.pallas_call(
        kern,
        out_shape=jax.ShapeDtypeStruct((T, D), F32),
        grid=(T // tm, D // tn),
        in_specs=[
            pl.BlockSpec((tm, MW), lambda m, n: (jnp.minimum(m, nct - 1), 0)),
            pl.BlockSpec((tm, MW), lambda m, n: (jnp.maximum(m - nct, 0), 0)),
            pl.BlockSpec((1, MW, tn), lambda m, n: (l, 0, n)),
            pl.BlockSpec((tm, tn), lambda m, n: (m, n)),
            pl.BlockSpec((1, N_MOD, D), lambda m, n: (_cond_index(m, tm, n_ctx_rows, dec_seq), 0, 0)),
        ],
        out_specs=pl.BlockSpec((tm, tn), lambda m, n: (m, n)),
        compiler_params=_cparams(("arbitrary", "arbitrary")),
        name="outproj",
    )(mix_c, mix_s, w_out, x, mods)


def _split_bf16(x):
    hi = x.astype(BF16)
    return hi, (x - hi.astype(F32)).astype(BF16)


def _router_kernel(x_ref, g_ref, mods_ref, wr_ref, br_ref, h_ref, idx_ref, gate_ref, *, n_experts):
    m = mods_ref[0]
    h = _rms_lanes(x_ref[...], g_ref[...]) * (1.0 + m[4:5]) + m[3:4]
    h_ref[...] = h
    h_hi, h_lo = _split_bf16(h)
    w_hi, w_lo = _split_bf16(wr_ref[...])
    logits = _dot(h_hi, w_hi) + _dot(h_hi, w_lo) + _dot(h_lo, w_hi) + br_ref[...]
    lane = lax.broadcasted_iota(jnp.int32, logits.shape, 1)
    lane_f = lane.astype(F32)
    neg = jnp.float32(-jnp.inf)
    logits = jnp.where(lane < n_experts, logits, neg)
    idx_out = jnp.zeros(logits.shape, F32)
    val_out = jnp.zeros(logits.shape, F32)
    top = None
    denom = None
    for k in range(TOP_K):
        v = jnp.max(logits, axis=-1, keepdims=True)
        i = jnp.min(jnp.where(logits == v, lane_f, float(LANES)), axis=-1, keepdims=True)
        if k == 0:
            top = v
        e = jnp.exp(v - top)
        denom = e if denom is None else denom + e
        idx_out = jnp.where(lane == k, i, idx_out)
        val_out = jnp.where(lane == k, e, val_out)
        logits = jnp.where(lane_f == i, neg, logits)
    idx_ref[...] = idx_out.astype(jnp.int32)
    gate_ref[...] = val_out / denom


def _router(x, g, mods, w_router_l, b_router_l, n_ctx_rows, dec_seq):
    T, D = x.shape
    E = w_router_l.shape[1]
    tm = 512
    wr = jnp.zeros((D, LANES), F32).at[:, :E].set(w_router_l)
    br = jnp.zeros((1, LANES), F32).at[0, :E].set(b_router_l)
    kern = functools.partial(_router_kernel, n_experts=E)
    return pl.pallas_call(
        kern,
        out_shape=(jax.ShapeDtypeStruct((T, D), F32),
                   jax.ShapeDtypeStruct((T, LANES), jnp.int32),
                   jax.ShapeDtypeStruct((T, LANES), F32)),
        grid=(T // tm,),
        in_specs=[
            pl.BlockSpec((tm, D), lambda m: (m, 0)),
            pl.BlockSpec((1, D), lambda m: (0, 0)),
            pl.BlockSpec((1, N_MOD, D), lambda m: (_cond_index(m, tm, n_ctx_rows, dec_seq), 0, 0)),
            pl.BlockSpec((D, LANES), lambda m: (0, 0)),
            pl.BlockSpec((1, LANES), lambda m: (0, 0)),
        ],
        out_specs=(pl.BlockSpec((tm, D), lambda m: (m, 0)),
                   pl.BlockSpec((tm, LANES), lambda m: (m, 0)),
                   pl.BlockSpec((tm, LANES), lambda m: (m, 0))),
        compiler_params=_cparams(("arbitrary",)),
        name="norm_router",
    )(x, g, mods, wr, br)


def _route_plan(top_idx, n_experts, n_sb):
    T, K = top_idx.shape
    flat_e = top_idx.reshape(-1)
    onehot = (flat_e[:, None] == jnp.arange(n_experts, dtype=jnp.int32)[None, :]).astype(jnp.int32)
    csum = jnp.cumsum(onehot, axis=0)
    counts = csum[-1]
    rank = jnp.take_along_axis(csum, flat_e[:, None], axis=1)[:, 0] - 1
    nblk = (counts + MOE_ROWS - 1) // MOE_ROWS
    blk_start = jnp.cumsum(nblk) - nblk
    pos = blk_start[flat_e] * MOE_ROWS + rank
    nsb = (nblk + SB_BLOCKS - 1) // SB_BLOCKS
    sb_end = jnp.cumsum(nsb)
    s = jnp.arange(n_sb, dtype=jnp.int32)
    total = sb_end[-1]
    sc = jnp.minimum(s, total - 1)
    e = jnp.sum((sb_end[None, :] <= sc[:, None]).astype(jnp.int32), axis=1)
    k = sc - (sb_end[e] - nsb[e])
    sb_blk0 = (blk_start[e] + k * SB_BLOCKS).astype(jnp.int32)
    sb_nblk = jnp.where(s < total, jnp.minimum(SB_BLOCKS, nblk[e] - k * SB_BLOCKS), 0).astype(jnp.int32)
    n_used = jnp.sum(nblk).astype(jnp.int32).reshape(1)
    plan = (e, sb_blk0, sb_nblk, n_used)
    return pos.reshape(T, K).astype(jnp.int32), plan, (blk_start.astype(jnp.int32), counts.astype(jnp.int32))


def _dispatch_kernel(start_ref, cnt_ref, n_used_ref, pos_ref, h_ref, xs_hbm, ring, zeros, sem, zsem):
    i = pl.program_id(0)
    n = pl.num_programs(0)
    tt = h_ref.shape[0]
    slot = i & 1

    ring[slot] = h_ref[...]

    def issue(t, c):
        for k in range(TOP_K):
            p = pos_ref[0, 0, t * TOP_K + k]
            pltpu.make_async_copy(ring.at[slot, pl.ds(t, 1)], xs_hbm.at[pl.ds(p, 1)], sem.at[slot]).start()
        return c

    lax.fori_loop(0, tt, issue, 0, unroll=2)

    def drain(sl):
        for _ in range(TOP_K):
            pltpu.make_async_copy(ring.at[sl], ring.at[sl], sem.at[sl]).wait()

    @pl.when(i == 0)
    def _():
        zeros[...] = jnp.zeros(zeros.shape, F32)

        def pad_copy(row):
            return pltpu.make_async_copy(zeros.at[pl.ds(0, 1)], xs_hbm.at[pl.ds(row, 1)], zsem)

        def tail_copy(b):
            dst = xs_hbm.at[pl.ds(pl.multiple_of(b * MOE_ROWS, MOE_ROWS), MOE_ROWS)]
            return pltpu.make_async_copy(zeros, dst, zsem)

        def per_expert(fn):
            def body(e, c):
                cnt = cnt_ref[e]
                first = start_ref[e] * MOE_ROWS + cnt
                lax.fori_loop(0, (-cnt) & (MOE_ROWS - 1), lambda j, cc: (fn(first + j), cc)[1], 0)
                return c
            lax.fori_loop(0, cnt_ref.shape[0], body, 0)

        def per_tail(fn):
            lax.fori_loop(n_used_ref[0], xs_hbm.shape[0] // MOE_ROWS, lambda b, cc: (fn(b), cc)[1], 0)

        per_expert(lambda row: pad_copy(row).start())
        per_tail(lambda b: tail_copy(b).start())
        per_expert(lambda row: pad_copy(row).wait())
        per_tail(lambda b: tail_copy(b).wait())

    @pl.when(i > 0)
    def _():
        drain(1 - slot)

    @pl.when(i == n - 1)
    def _():
        drain(slot)


def _dispatch(h, pos, pad_info, n_used, n_slots):
    T, D = h.shape
    blk_start, counts = pad_info
    tt = 256
    grid_spec = pltpu.PrefetchScalarGridSpec(
        num_scalar_prefetch=3,
        grid=(T // tt,),
        in_specs=[
            pl.BlockSpec((1, 1, tt * TOP_K), lambda i, s, c, u: (i, 0, 0), memory_space=pltpu.SMEM),
            pl.BlockSpec((tt, D), lambda i, s, c, u: (i, 0)),
        ],
        out_specs=pl.BlockSpec(memory_space=pl.ANY),
        scratch_shapes=[pltpu.VMEM((2, tt, D), F32), pltpu.VMEM((MOE_ROWS, D), F32),
                        pltpu.SemaphoreType.DMA((2,)), pltpu.SemaphoreType.DMA],
    )
    return pl.pallas_call(
        _dispatch_kernel,
        out_shape=jax.ShapeDtypeStruct((n_slots, D), F32),
        grid_spec=grid_spec,
        compiler_params=_cparams(("arbitrary",)),
        name="moe_dispatch",
    )(blk_start, counts, n_used, pos.reshape(T // tt, 1, tt * TOP_K), h)


def _moe_kernel(sb_e_ref, sb_blk0_ref, sb_nblk_ref, n_used_ref, xs_hbm, wg_ref, bg_ref, wu_ref, bu_ref,
                wd_ref, bd_ref, y_hbm, x_s, y_s, wgu_s, wd_s, stage, sem_in, sem_out):
    s = pl.program_id(0)
    f = pl.program_id(1)
    n_f = pl.num_programs(1)
    nblk = sb_nblk_ref[s]
    blk0 = sb_blk0_ref[s]
    tf = wg_ref.shape[3]

    n_stage = stage.shape[0]

    def in_copy(r):
        src = xs_hbm.at[pl.ds(pl.multiple_of((blk0 + r) * MOE_ROWS, MOE_ROWS), MOE_ROWS)]
        return pltpu.make_async_copy(src, stage.at[r & (n_stage - 1)], sem_in.at[r & (n_stage - 1)])

    def out_copy(r):
        dst = y_hbm.at[pl.ds(pl.multiple_of((blk0 + r) * MOE_ROWS, MOE_ROWS), MOE_ROWS)]
        return pltpu.make_async_copy(y_s.at[pl.ds(pl.multiple_of(r * MOE_ROWS, MOE_ROWS), MOE_ROWS)], dst, sem_out)

    def each_block(fn):
        def body(r, c):
            fn(r)
            return c
        lax.fori_loop(0, nblk, body, 0)

    @pl.when(nblk > 0)
    def _():
        @pl.when(f == 0)
        def _():
            lax.fori_loop(0, jnp.minimum(nblk, n_stage), lambda r, c: (in_copy(r).start(), c)[1], 0)

        wgu_s[:, :tf] = wg_ref[0, 0].astype(BF16)
        wgu_s[:, tf:] = wu_ref[0, 0].astype(BF16)
        wd_s[...] = wd_ref[0, 0].astype(BF16)
        bg = bg_ref[0, 0]
        bu = bu_ref[0, 0]
        bd = bd_ref[0, 0]

        @pl.when(f == 0)
        def _():
            bias_rows = jnp.broadcast_to(bd, (MOE_ROWS, bd.shape[1]))

            def load(r):
                rows = pl.ds(pl.multiple_of(r * MOE_ROWS, MOE_ROWS), MOE_ROWS)
                in_copy(r).wait()
                x_s[rows, :] = stage[r & (n_stage - 1)].astype(BF16)

                @pl.when(r + n_stage < nblk)
                def _():
                    in_copy(r + n_stage).start()

                y_s[rows, :] = bias_rows

            each_block(load)

        def ffn(blk, nb):
            rows = pl.ds(pl.multiple_of(blk * MOE_ROWS, MOE_ROWS), nb * MOE_ROWS)
            gu = _dot(x_s[rows, :], wgu_s[...])
            g = jnp.minimum(gu[:, :tf] + bg, SWIGLU_LIMIT)
            u = jnp.clip(gu[:, tf:] + bu, -SWIGLU_LIMIT, SWIGLU_LIMIT)
            act = (u + 1.0) * g * jax.nn.sigmoid(SWIGLU_ALPHA * g)
            y_s[rows, :] += _dot(act.astype(BF16), wd_s[...])

        n_big = lax.shift_right_logical(nblk, FFN_CHUNK_LOG2)

        def big(c, carry):
            ffn(c * FFN_CHUNK, FFN_CHUNK)
            return carry

        lax.fori_loop(0, n_big, big, 0)
        done = n_big * FFN_CHUNK
        nb = FFN_CHUNK // 2
        while nb >= 1:
            @pl.when((nblk & nb) != 0)
            def _(done=done, nb=nb):
                ffn(done, nb)
            done = done + (nblk & nb)
            nb //= 2

        @pl.when(f == n_f - 1)
        def _():
            each_block(lambda r: out_copy(r).start())
            each_block(lambda r: out_copy(r).wait())

    @pl.when(jnp.logical_and(s == pl.num_programs(0) - 1, f == n_f - 1))
    def _():
        y_s[0:MOE_ROWS, :] = jnp.zeros((MOE_ROWS, y_s.shape[1]), F32)

        def tail_copy(r):
            dst = y_hbm.at[pl.ds(pl.multiple_of(r * MOE_ROWS, MOE_ROWS), MOE_ROWS)]
            return pltpu.make_async_copy(y_s.at[0:MOE_ROWS], dst, sem_out)

        def start(r, c):
            tail_copy(r).start()
            return c

        def wait(r, c):
            tail_copy(r).wait()
            return c

        lax.fori_loop(n_used_ref[0], y_hbm.shape[0] // MOE_ROWS, start, 0)
        lax.fori_loop(n_used_ref[0], y_hbm.shape[0] // MOE_ROWS, wait, 0)


def _moe_ffn(xs, plan, w_gate, b_gate, w_up, b_up, w_down, b_down, l, n_sb):
    sb_e, sb_blk0, sb_nblk, n_used = plan
    n_slots, D = xs.shape
    L, E, _, F = w_gate.shape
    tf = 256
    rows = SB_BLOCKS * MOE_ROWS
    n_f = F // tf

    def ftile(s, f, n):
        return jnp.where(n[s] > 0, f, n_f - 1)

    grid_spec = pltpu.PrefetchScalarGridSpec(
        num_scalar_prefetch=4,
        grid=(n_sb, n_f),
        in_specs=[
            pl.BlockSpec(memory_space=pl.ANY),
            pl.BlockSpec((1, 1, D, tf), lambda s, f, e, b, n, u: (l, e[s], 0, ftile(s, f, n))),
            pl.BlockSpec((1, 1, 1, tf), lambda s, f, e, b, n, u: (l, e[s], 0, ftile(s, f, n))),
            pl.BlockSpec((1, 1, D, tf), lambda s, f, e, b, n, u: (l, e[s], 0, ftile(s, f, n))),
            pl.BlockSpec((1, 1, 1, tf), lambda s, f, e, b, n, u: (l, e[s], 0, ftile(s, f, n))),
            pl.BlockSpec((1, 1, tf, D), lambda s, f, e, b, n, u: (l, e[s], ftile(s, f, n), 0)),
            pl.BlockSpec((1, 1, 1, D), lambda s, f, e, b, n, u: (l, e[s], 0, 0)),
        ],
        out_specs=pl.BlockSpec(memory_space=pl.ANY),
        scratch_shapes=[
            pltpu.VMEM((rows, D), BF16),
            pltpu.VMEM((rows, D), F32),
            pltpu.VMEM((D, 2 * tf), BF16),
            pltpu.VMEM((tf, D), BF16),
            pltpu.VMEM((XS_STAGES, MOE_ROWS, D), F32),
            pltpu.SemaphoreType.DMA((XS_STAGES,)),
            pltpu.SemaphoreType.DMA,
        ],
    )
    return pl.pallas_call(
        _moe_kernel,
        out_shape=jax.ShapeDtypeStruct((n_slots, D), F32),
        grid_spec=grid_spec,
        compiler_params=_cparams(("arbitrary", "arbitrary")),
        name="moe_ffn",
    )(sb_e, sb_blk0, sb_nblk, n_used, xs, w_gate, b_gate.reshape(L, E, 1, F), w_up, b_up.reshape(L, E, 1, F),
      w_down, b_down.reshape(L, E, 1, D))


def _combine_kernel(pos_ref, pos_next_ref, y_hbm, gates_ref, x_ref, mods_ref, o_ref, stage, sem):
    i = pl.program_id(0)
    n_steps = pl.num_programs(0)
    n = x_ref.shape[0]
    slot = i & 1

    def fetch(p_ref, sl):
        def issue(t, c):
            for k in range(TOP_K):
                p = p_ref[0, 0, t * TOP_K + k]
                pltpu.make_async_copy(y_hbm.at[pl.ds(p, 1)], stage.at[sl, k, pl.ds(t, 1)], sem.at[sl]).start()
            return c
        lax.fori_loop(0, n, issue, 0, unroll=2)

    @pl.when(i == 0)
    def _():
        fetch(pos_ref, slot)

    @pl.when(i + 1 < n_steps)
    def _():
        fetch(pos_next_ref, 1 - slot)

    for k in range(TOP_K):
        pltpu.make_async_copy(stage.at[slot, k], stage.at[slot, k], sem.at[slot]).wait()
    gates = gates_ref[...]
    acc = gates[:, 0:1] * stage[slot, 0]
    for k in range(1, TOP_K):
        acc = acc + gates[:, k:k + 1] * stage[slot, k]
    o_ref[...] = x_ref[...] + mods_ref[0][5:6] * acc


def _combine(y_slots, pos, gates, x, mods, n_ctx_rows, dec_seq):
    T, D = x.shape
    cb = 128
    n_steps = T // cb
    return pl.pallas_call(
        _combine_kernel,
        out_shape=jax.ShapeDtypeStruct((T, D), F32),
        grid=(n_steps,),
        in_specs=[
            pl.BlockSpec((1, 1, cb * TOP_K), lambda i: (i, 0, 0), memory_space=pltpu.SMEM),
            pl.BlockSpec((1, 1, cb * TOP_K), lambda i: (jnp.minimum(i + 1, n_steps - 1), 0, 0),
                         memory_space=pltpu.SMEM),
            pl.BlockSpec(memory_space=pl.ANY),
            pl.BlockSpec((cb, LANES), lambda i: (i, 0)),
            pl.BlockSpec((cb, D), lambda i: (i, 0)),
            pl.BlockSpec((1, N_MOD, D), lambda i: (_cond_index(i, cb, n_ctx_rows, dec_seq), 0, 0)),
        ],
        out_specs=pl.BlockSpec((cb, D), lambda i: (i, 0)),
        scratch_shapes=[pltpu.VMEM((2, TOP_K, cb, D), F32), pltpu.SemaphoreType.DMA((2,))],
        compiler_params=_cparams(("arbitrary",)),
        name="moe_combine",
    )(pos.reshape(n_steps, 1, cb * TOP_K), pos.reshape(n_steps, 1, cb * TOP_K), y_slots, gates, x, mods)


def _final_norm_kernel(x_ref, g_ref, o_ref):
    o_ref[...] = _rms_lanes(x_ref[...], g_ref[...])


def _final_norm(x, g):
    T, D = x.shape
    tm = 512
    return pl.pallas_call(
        _final_norm_kernel,
        out_shape=jax.ShapeDtypeStruct((T, D), F32),
        grid=(T // tm,),
        in_specs=[pl.BlockSpec((tm, D), lambda m: (m, 0)), pl.BlockSpec((1, D), lambda m: (0, 0))],
        out_specs=pl.BlockSpec((tm, D), lambda m: (m, 0)),
        compiler_params=_cparams(("arbitrary",)),
        name="final_norm",
    )(x, g)


def _rope_tables(n, head_dim):
    quarter = head_dim // 4
    inv = ROPE_THETA ** (-jnp.arange(quarter, dtype=F32) / quarter)
    pos = jnp.arange(n)
    row = (pos // GRID_W).astype(F32)[:, None] * inv
    col = (pos % GRID_W).astype(F32)[:, None] * inv
    cos = jnp.concatenate([jnp.cos(row), jnp.cos(row), jnp.cos(col), jnp.cos(col)], axis=1)
    sin = jnp.concatenate([-jnp.sin(row), jnp.sin(row), -jnp.sin(col), jnp.sin(col)], axis=1)
    reps = LANES // head_dim
    return jnp.tile(cos, (1, reps)), jnp.tile(sin, (1, reps))


def kernel(x_prompt, x_sample, c, cache_k_a, cache_v_a, cache_k_c, cache_v_c, c_ctx, w_mod, b_mod, g_norm1, g_norm2, w_in, lam_q1, lam_k1, lam_q2, lam_k2, g_subln_a, g_q_c, g_k_c, g_v_b, w_s_b, b_s_b, w_d, s_d, w_out, w_router, b_router, w_gate, b_gate, w_up, b_up, w_down, b_down, g_final):
    B, S, D = x_prompt.shape
    DB, DS, _ = x_sample.shape
    L = w_mod.shape[0]
    P = cache_k_a.shape[2]
    E = w_router.shape[2]
    n_ctx = B * S
    T = n_ctx + DB * DS
    GW = w_out.shape[1] // 4
    dk_a = cache_k_a.shape[-1]
    hd_c = cache_k_c.shape[-1]

    x = jnp.concatenate([x_prompt.reshape(n_ctx, D), x_sample.reshape(DB * DS, D)], axis=0)
    cond8 = jnp.zeros((8, D), F32).at[0].set(c_ctx).at[1:1 + DB].set(c)
    mod = _modulation(cond8, w_mod, b_mod).reshape(L, 8, N_MOD, D)

    cos_a, sin_a = _rope_tables(DS, dk_a)
    cos_c, sin_c = _rope_tables(DS, hd_c)
    caches_all = (cache_k_a.reshape(DB, L, P, GW), cache_v_a.reshape(DB, L, P, GW),
                  cache_k_c.reshape(DB, L, P, GW // 2), cache_v_c.reshape(DB, L, P, GW // 2))

    w_in = _to_bf16(w_in)
    w_out = _to_bf16(w_out)
    n_slots_max = -(-(T * TOP_K + E * (MOE_ROWS - 1)) // MOE_ROWS)
    n_sb = E + n_slots_max // SB_BLOCKS

    states = []
    for l in range(L):
        mods = mod[l]
        lam0 = 0.8 - 0.6 * math.exp(-0.3 * l)
        proj = _inproj(x, g_norm1[l][None, :], mods, w_in, l, n_ctx, DS)
        params = (jnp.stack([lam_q1[l], lam_k1[l], lam_q2[l], lam_k2[l]]), g_subln_a[l][None, :],
                  g_q_c[l][None, :], g_k_c[l][None, :], g_v_b[l][None, :], w_s_b[l], b_s_b[l].T,
                  w_d[l], s_d[l][None, :])
        mix_c, kc_norm = _mixer(proj, l, 0, B, S, lam0, params)
        mix_s = _mixer(proj, l, n_ctx, DB, DS, lam0, params, caches=caches_all,
                       rope_tabs=(cos_a, sin_a, cos_c, sin_c))
        pc = proj[:n_ctx]
        states.append((pc[:, GW:2 * GW], pc[:, 2 * GW:3 * GW], kc_norm, pc[:, 6 * GW + GW // 2:7 * GW]))
        x = _outproj(mix_c, mix_s, w_out, x, mods, l, n_ctx, DS)
        h, top_idx, gates = _router(x, g_norm2[l][None, :], mods, w_router[l], b_router[l], n_ctx, DS)
        pos, plan, pad_info = _route_plan(top_idx[:, :TOP_K], E, n_sb)
        xs = _dispatch(h, pos, pad_info, plan[3], n_slots_max * MOE_ROWS)
        y_slots = _moe_ffn(xs, plan, w_gate, b_gate, w_up, b_up, w_down, b_down, l, n_sb)
        x = _combine(y_slots, pos, gates, x, mods, n_ctx, DS)

    y = _final_norm(x, g_final[None, :])
    y_prompt = y[:n_ctx].reshape(B, S, D)
    y_sample = y[n_ctx:].reshape(DB, DS, D)
    h_a = GW // (2 * dk_a)
    new_k_a = jnp.stack([st[0].reshape(B, S, h_a, 2, dk_a) for st in states], axis=1)
    new_v_a = jnp.stack([st[1].reshape(B, S, h_a, 2 * dk_a) for st in states], axis=1)
    new_k_c = jnp.stack([st[2].reshape(B, S, GW // 2 // hd_c, hd_c) for st in states], axis=1)
    new_v_c = jnp.stack([st[3].reshape(B, S, GW // 2 // hd_c, hd_c) for st in states], axis=1)
    return (y_prompt, y_sample, new_k_a, new_v_a, new_k_c, new_v_c)
```

```python
import functools
import math

import jax
import jax.numpy as jnp
from jax import lax
from jax.experimental import pallas as pl
from jax.experimental.pallas import tpu as pltpu

F32 = jnp.float32
BF16 = jnp.bfloat16

EPS = 1e-6
GRID_W = 64
ROPE_THETA = 10000.0
N_MOD = 6
TOP_K = 4
SWIGLU_LIMIT = 7.0
SWIGLU_ALPHA = 1.702
POOL_WINDOWS = (2, 4, 8, 16)

LANES = 128
QB = 128
MOE_ROWS = 128
SB_BLOCKS = 16
XS_STAGES = 8
FFN_CHUNK_LOG2 = 2
FFN_CHUNK = 1 << FFN_CHUNK_LOG2
VMEM_LIMIT = 56 * 1024 * 1024


def _cparams(sem, vmem=VMEM_LIMIT):
    return pltpu.CompilerParams(dimension_semantics=sem, vmem_limit_bytes=vmem)


def _dot(a, b):
    return jnp.dot(a, b, preferred_element_type=F32)


def _dot_nt(a, b):
    return lax.dot_general(a, b, (((1,), (1,)), ((), ())), preferred_element_type=F32)


def _rms_lanes(x, g):
    ms = jnp.mean(x * x, axis=-1, keepdims=True)
    return x * lax.rsqrt(ms + EPS) * g


def _mod_kernel(cond_ref, w_ref, b_ref, o_ref):
    c = cond_ref[...]
    a = (c * jax.nn.sigmoid(c)).astype(BF16)
    o_ref[0] = _dot(a, w_ref[0].astype(BF16)) + b_ref[0]


def _modulation(cond8, w_mod, b_mod):
    L, D, NM = w_mod.shape
    tn = 1024
    return pl.pallas_call(
        _mod_kernel,
        out_shape=jax.ShapeDtypeStruct((L, 8, NM), F32),
        grid=(L, NM // tn),
        in_specs=[
            pl.BlockSpec((8, D), lambda l, n: (0, 0)),
            pl.BlockSpec((1, D, tn), lambda l, n: (l, 0, n)),
            pl.BlockSpec((1, 1, tn), lambda l, n: (l, 0, n)),
        ],
        out_specs=pl.BlockSpec((1, 8, tn), lambda l, n: (l, 0, n)),
        compiler_params=_cparams(("arbitrary", "arbitrary")),
        name="modulation",
    )(cond8, w_mod, b_mod.reshape(L, 1, NM))


def _cast_kernel(w_ref, o_ref):
    o_ref[...] = w_ref[...].astype(BF16)


def _to_bf16(w):
    L, K, N = w.shape
    tn = 512
    return pl.pallas_call(
        _cast_kernel,
        out_shape=jax.ShapeDtypeStruct(w.shape, BF16),
        grid=(L, N // tn),
        in_specs=[pl.BlockSpec((1, K, tn), lambda l, n: (l, 0, n))],
        out_specs=pl.BlockSpec((1, K, tn), lambda l, n: (l, 0, n)),
        compiler_params=_cparams(("arbitrary", "arbitrary")),
        name="weights_to_bf16",
    )(w)


def _cond_index(row_block, rows_per_block, n_ctx_rows, dec_seq):
    r0 = row_block * rows_per_block
    return jnp.where(r0 < n_ctx_rows, 0, 1 + (r0 - n_ctx_rows) // dec_seq)


def _inproj_kernel(x_ref, g_ref, mods_ref, w_ref, o_ref, h_ref, *, shift_row, scale_row, chunk):
    @pl.when(pl.program_id(1) == 0)
    def _():
        m = mods_ref[0]
        shift = m[shift_row:shift_row + 1]
        scale1 = 1.0 + m[scale_row:scale_row + 1]
        g = g_ref[...]

        def body(i, carry):
            r = pl.multiple_of(i * chunk, chunk)
            x = x_ref[pl.ds(r, chunk), :]
            h_ref[pl.ds(r, chunk), :] = (_rms_lanes(x, g) * scale1 + shift).astype(BF16)
            return carry

        lax.fori_loop(0, x_ref.shape[0] // chunk, body, 0)

    o_ref[...] = _dot(h_ref[...], w_ref[0])


def _inproj(x, g, mods, w_in, l, n_ctx_rows, dec_seq):
    T, D = x.shape
    PW = w_in.shape[2]
    tm, tn = 1024, 1024
    kern = functools.partial(_inproj_kernel, shift_row=0, scale_row=1, chunk=128)
    return pl.pallas_call(
        kern,
        out_shape=jax.ShapeDtypeStruct((T, PW), F32),
        grid=(T // tm, PW // tn),
        in_specs=[
            pl.BlockSpec((tm, D), lambda m, n: (m, 0)),
            pl.BlockSpec((1, D), lambda m, n: (0, 0)),
            pl.BlockSpec((1, N_MOD, D), lambda m, n: (_cond_index(m, tm, n_ctx_rows, dec_seq), 0, 0)),
            pl.BlockSpec((1, D, tn), lambda m, n: (l, 0, n)),
        ],
        out_specs=pl.BlockSpec((tm, tn), lambda m, n: (m, n)),
        scratch_shapes=[pltpu.VMEM((tm, D), BF16)],
        compiler_params=_cparams(("arbitrary", "arbitrary")),
        name="inproj",
    )(x, g, mods, w_in)


def _rope(x, cos, sin, q):
    lane = lax.broadcasted_iota(jnp.int32, x.shape, 1)
    first_half = (lane & q) == 0
    partner = jnp.where(first_half, pltpu.roll(x, LANES - q, 1), pltpu.roll(x, q, 1))
    return x * cos + partner * sin


def _softmax_parts(s):
    m = jnp.max(s, axis=-1, keepdims=True)
    e = jnp.exp(s - m)
    return e, jnp.sum(e, axis=-1, keepdims=True)


def _mixer_kernel(*refs, seq, n_cache, lam0, rope):
    it = iter(refs)
    qa_ref, ub_ref, vb_ref, qc_ref = next(it), next(it), next(it), next(it)
    ka_ref, va_ref, kv_ref, pd_ref = next(it), next(it), next(it), next(it)
    if n_cache:
        cka_ref, cva_ref, ckc_ref, cvc_ref = next(it), next(it), next(it), next(it)
    if rope:
        cos_a_ref, sin_a_ref, cos_c_ref, sin_c_ref = next(it), next(it), next(it), next(it)
    lamp_ref, gsub_ref, gq_ref, gk_ref, gvb_ref = next(it), next(it), next(it), next(it), next(it)
    ws_ref, bst_ref, wd_ref, sd_ref = next(it), next(it), next(it), next(it)
    mix_ref = next(it)
    kcn_ref = None if n_cache else next(it)
    ka_s, va_s, kc_s, vc_s = next(it), next(it), next(it), next(it)

    j = pl.program_id(1)
    row0 = pl.multiple_of(j * QB, QB)
    n_heads = qa_ref.shape[1] // LANES
    n_kv = kc_s.shape[1] // LANES
    dk_a = LANES // 2

    @pl.when(j == 0)
    def _():
        prep = 256
        for c in range(seq // prep):
            rows = slice(c * prep, (c + 1) * prep)
            for b in range(n_heads):
                cols = slice(b * LANES, (b + 1) * LANES)
                k = ka_ref[rows, cols]
                if rope:
                    k = _rope(k, cos_a_ref[rows, :], sin_a_ref[rows, :], dk_a // 4)
                ka_s[rows, cols] = k.astype(BF16)
                va_s[rows, cols] = va_ref[rows, cols].astype(BF16)
            for b in range(n_kv):
                cols = slice(b * LANES, (b + 1) * LANES)
                k = _rms_lanes(kv_ref[rows, cols], gk_ref[...])
                if kcn_ref is not None:
                    kcn_ref[rows, cols] = k
                if rope:
                    k = _rope(k, cos_c_ref[rows, :], sin_c_ref[rows, :], LANES // 4)
                kc_s[rows, cols] = k.astype(BF16)
                vcols = slice((n_kv + b) * LANES, (n_kv + b + 1) * LANES)
                vc_s[rows, cols] = kv_ref[rows, vcols].astype(BF16)
        if n_cache:
            tail = slice(seq, seq + n_cache)
            ka_s[tail, :] = cka_ref[0].astype(BF16)
            va_s[tail, :] = cva_ref[0].astype(BF16)
            kc_s[tail, :] = ckc_ref[0].astype(BF16)
            vc_s[tail, :] = cvc_ref[0].astype(BF16)

    lane = lax.broadcasted_iota(jnp.int32, (QB, LANES), 1)

    lp = lamp_ref[...]
    lam = (jnp.exp(jnp.sum(lp[0:1] * lp[1:2], axis=-1, keepdims=True))
           - jnp.exp(jnp.sum(lp[2:3] * lp[3:4], axis=-1, keepdims=True)) + lam0)
    scale_a = dk_a ** -0.5
    if rope:
        cos_a = cos_a_ref[pl.ds(row0, QB), :]
        sin_a = sin_a_ref[pl.ds(row0, QB), :]
        cos_c = cos_c_ref[pl.ds(row0, QB), :]
        sin_c = sin_c_ref[pl.ds(row0, QB), :]
    for h in range(n_heads):
        cols = slice(h * LANES, (h + 1) * LANES)
        q = qa_ref[:, cols]
        if rope:
            q = _rope(q, cos_a, sin_a, dk_a // 4)
        k = ka_s[:, cols]
        q0 = jnp.where(lane < dk_a, q, 0.0).astype(BF16)
        q1 = jnp.where(lane >= dk_a, q, 0.0).astype(BF16)
        e0, l0 = _softmax_parts(_dot_nt(q0, k) * scale_a)
        e1, l1 = _softmax_parts(_dot_nt(q1, k) * scale_a)
        a = e0 * (1.0 / l0) - e1 * (lam / l1)
        o = _dot(a.astype(BF16), va_s[:, cols])
        mix_ref[:, cols] = (_rms_lanes(o, gsub_ref[...]) * (1.0 - lam0)).astype(BF16)

    off_b = n_heads * LANES
    for g in range(ub_ref.shape[1] // LANES):
        cols = slice(g * LANES, (g + 1) * LANES)
        vn = _rms_lanes(vb_ref[:, cols], gvb_ref[:, cols])
        mixed = _dot(ws_ref[g].astype(BF16), vn.astype(BF16)) + bst_ref[:, g:g + 1]
        mix_ref[:, off_b + g * LANES: off_b + (g + 1) * LANES] = (ub_ref[:, cols] * mixed).astype(BF16)

    off_c = off_b + ub_ref.shape[1]
    scale_c = LANES ** -0.5
    for h in range(n_heads):
        cols = slice(h * LANES, (h + 1) * LANES)
        q = _rms_lanes(qc_ref[:, cols], gq_ref[...])
        if rope:
            q = _rope(q, cos_c, sin_c, LANES // 4)
        kv = h // (n_heads // n_kv)
        kcols = slice(kv * LANES, (kv + 1) * LANES)
        e, l = _softmax_parts(_dot_nt(q.astype(BF16), kc_s[:, kcols]) * scale_c)
        p = e * (1.0 / l)
        o = _dot(p.astype(BF16), vc_s[:, kcols])
        mix_ref[:, off_c + h * LANES: off_c + (h + 1) * LANES] = o.astype(BF16)

    off_d = off_c + qc_ref.shape[1]
    win = min(3 * QB, seq)
    start = pl.multiple_of(jnp.clip(row0 - QB, 0, seq - win), QB)
    t = row0 + lax.broadcasted_iota(jnp.int32, (QB, win), 0)
    col = start + lax.broadcasted_iota(jnp.int32, (QB, win), 1)
    t1 = row0 + lax.broadcasted_iota(jnp.int32, (QB, 1), 0)
    for g, w in enumerate(POOL_WINDOWS):
        cols = slice(g * LANES, (g + 1) * LANES)
        left, right = w // 2, w - 1 - w // 2
        band = jnp.where((col >= t - left) & (col <= t + right), 1.0, 0.0).astype(BF16)
        cnt = (jnp.minimum(t1 + right + 1, seq) - jnp.maximum(t1 - left, 0)).astype(F32)
        xw = pd_ref[pl.ds(start, win), cols]
        hi = xw.astype(BF16)
        lo = (xw - hi.astype(F32)).astype(BF16)
        wsum = _dot(band, hi) + _dot(band, lo)
        xg = pd_ref[pl.ds(row0, QB), cols]
        pooled = wsum / cnt - xg
        y = _dot(pooled.astype(BF16), wd_ref[g].astype(BF16)) * sd_ref[:, cols]
        mix_ref[:, off_d + g * LANES: off_d + (g + 1) * LANES] = y.astype(BF16)


def _mixer(proj, l, row_start, n_seq, seq, lam0, params, caches=None, rope_tabs=None):
    T, PW = proj.shape
    GW = PW // 8
    n_qb = seq // QB
    qb0 = row_start // QB
    sb0 = row_start // seq
    n_cache = 0 if caches is None else caches[0].shape[2]
    n_keys = seq + n_cache

    def qspec(col):
        return pl.BlockSpec((QB, GW), lambda s, j: (qb0 + s * n_qb + j, col))

    def sspec(col):
        return pl.BlockSpec((seq, GW), lambda s, j: (sb0 + s, col))

    def full(shape):
        nd = len(shape)
        return pl.BlockSpec(shape, lambda s, j: (0,) * nd)

    in_specs = [qspec(0), qspec(3), qspec(4), qspec(5), sspec(1), sspec(2), sspec(6), sspec(7)]
    args = [proj] * 8
    if caches is not None:
        for c in caches:
            in_specs.append(pl.BlockSpec((None, 1, c.shape[2], c.shape[3]), lambda s, j: (s, l, 0, 0)))
            args.append(c)
    if rope_tabs is not None:
        for tab in rope_tabs:
            in_specs.append(full(tab.shape))
            args.append(tab)
    for p in params:
        in_specs.append(full(p.shape))
        args.append(p)

    MW = 4 * GW
    mix_spec = pl.BlockSpec((QB, MW), lambda s, j: (s * n_qb + j, 0))
    if caches is None:
        out_shape = (jax.ShapeDtypeStruct((n_seq * seq, MW), BF16),
                     jax.ShapeDtypeStruct((n_seq * seq, GW // 2), F32))
        out_specs = (mix_spec, pl.BlockSpec((seq, GW // 2), lambda s, j: (s, 0)))
    else:
        out_shape = jax.ShapeDtypeStruct((n_seq * seq, MW), BF16)
        out_specs = mix_spec

    kern = functools.partial(_mixer_kernel, seq=seq, n_cache=n_cache, lam0=lam0,
                             rope=rope_tabs is not None)
    return pl.pallas_call(
        kern,
        out_shape=out_shape,
        grid=(n_seq, n_qb),
        in_specs=in_specs,
        out_specs=out_specs,
        scratch_shapes=[pltpu.VMEM((n_keys, GW), BF16), pltpu.VMEM((n_keys, GW), BF16),
                        pltpu.VMEM((n_keys, GW // 2), BF16), pltpu.VMEM((n_keys, GW // 2), BF16)],
        compiler_params=_cparams(("arbitrary", "arbitrary")),
        name="mixer_latent" if caches is not None else "mixer_context",
    )(*args)


def _outproj_kernel(mix_c_ref, mix_s_ref, w_ref, x_ref, mods_ref, o_ref, *, gate_row, n_ctx_tiles):
    tn = o_ref.shape[1]
    gcols = pl.ds(pl.multiple_of(pl.program_id(1) * tn, tn), tn)
    gate = mods_ref[0, gate_row:gate_row + 1, gcols]
    w = w_ref[0]
    is_ctx = pl.program_id(0) < n_ctx_tiles

    @pl.when(is_ctx)
    def _():
        o_ref[...] = x_ref[...] + gate * _dot(mix_c_ref[...], w)

    @pl.when(jnp.logical_not(is_ctx))
    def _():
        o_ref[...] = x_ref[...] + gate * _dot(mix_s_ref[...], w)


def _outproj(mix_c, mix_s, w_out, x, mods, l, n_ctx_rows, dec_seq):
    T, D = x.shape
    MW = mix_c.shape[1]
    tm, tn = 1024, 1024
    nct = n_ctx_rows // tm
    kern = functools.partial(_outproj_kernel, gate_row=2, n_ctx_tiles=nct)
    return pl.pallas_call(
        kern,
        out_shape=jax.ShapeDtypeStruct((T, D), F32),
        grid=(T // tm, D // tn),
        in_specs=[
            pl.BlockSpec((tm, MW), lambda m, n: (jnp.minimum(m, nct - 1), 0)),
            pl.BlockSpec((tm, MW), lambda m, n: (jnp.maximum(m - nct, 0), 0)),
            pl.BlockSpec((1, MW, tn), lambda m, n: (l, 0, n)),
            pl.BlockSpec((tm, tn), lambda m, n: (m, n)),
            pl.BlockSpec((1, N_MOD, D), lambda m, n: (_cond_index(m, tm, n_ctx_rows, dec_seq), 0, 0)),
        ],
        out_specs=pl.BlockSpec((tm, tn), lambda m, n: (m, n)),
        compiler_params=_cparams(("arbitrary", "arbitrary")),
        name="outproj",
    )(mix_c, mix_s, w_out, x, mods)


def _split_bf16(x):
    hi = x.astype(BF16)
    return hi, (x - hi.astype(F32)).astype(BF16)


def _router_kernel(x_ref, g_ref, mods_ref, wr_ref, br_ref, h_ref, idx_ref, gate_ref, *, n_experts):
    m = mods_ref[0]
    h = _rms_lanes(x_ref[...], g_ref[...]) * (1.0 + m[4:5]) + m[3:4]
    h_ref[...] = h
    h_hi, h_lo = _split_bf16(h)
    w_hi, w_lo = _split_bf16(wr_ref[...])
    logits = _dot(h_hi, w_hi) + _dot(h_hi, w_lo) + _dot(h_lo, w_hi) + br_ref[...]
    lane = lax.broadcasted_iota(jnp.int32, logits.shape, 1)
    lane_f = lane.astype(F32)
    neg = jnp.float32(-jnp.inf)
    logits = jnp.where(lane < n_experts, logits, neg)
    idx_out = jnp.zeros(logits.shape, F32)
    val_out = jnp.zeros(logits.shape, F32)
    top = None
    denom = None
    for k in range(TOP_K):
        v = jnp.max(logits, axis=-1, keepdims=True)
        i = jnp.min(jnp.where(logits == v, lane_f, float(LANES)), axis=-1, keepdims=True)
        if k == 0:
            top = v
        e = jnp.exp(v - top)
        denom = e if denom is None else denom + e
        idx_out = jnp.where(lane == k, i, idx_out)
        val_out = jnp.where(lane == k, e, val_out)
        logits = jnp.where(lane_f == i, neg, logits)
    idx_ref[...] = idx_out.astype(jnp.int32)
    gate_ref[...] = val_out / denom


def _router(x, g, mods, w_router_l, b_router_l, n_ctx_rows, dec_seq):
    T, D = x.shape
    E = w_router_l.shape[1]
    tm = 512
    wr = jnp.zeros((D, LANES), F32).at[:, :E].set(w_router_l)
    br = jnp.zeros((1, LANES), F32).at[0, :E].set(b_router_l)
    kern = functools.partial(_router_kernel, n_experts=E)
    return pl.pallas_call(
        kern,
        out_shape=(jax.ShapeDtypeStruct((T, D), F32),
                   jax.ShapeDtypeStruct((T, LANES), jnp.int32),
                   jax.ShapeDtypeStruct((T, LANES), F32)),
        grid=(T // tm,),
        in_specs=[
            pl.BlockSpec((tm, D), lambda m: (m, 0)),
            pl.BlockSpec((1, D), lambda m: (0, 0)),
            pl.BlockSpec((1, N_MOD, D), lambda m: (_cond_index(m, tm, n_ctx_rows, dec_seq), 0, 0)),
            pl.BlockSpec((D, LANES), lambda m: (0, 0)),
            pl.BlockSpec((1, LANES), lambda m: (0, 0)),
        ],
        out_specs=(pl.BlockSpec((tm, D), lambda m: (m, 0)),
                   pl.BlockSpec((tm, LANES), lambda m: (m, 0)),
                   pl.BlockSpec((tm, LANES), lambda m: (m, 0))),
        compiler_params=_cparams(("arbitrary",)),
        name="norm_router",
    )(x, g, mods, wr, br)


def _route_plan(top_idx, n_experts, n_sb):
    T, K = top_idx.shape
    flat_e = top_idx.reshape(-1)
    onehot = (flat_e[:, None] == jnp.arange(n_experts, dtype=jnp.int32)[None, :]).astype(jnp.int32)
    csum = jnp.cumsum(onehot, axis=0)
    counts = csum[-1]
    rank = jnp.take_along_axis(csum, flat_e[:, None], axis=1)[:, 0] - 1
    nblk = (counts + MOE_ROWS - 1) // MOE_ROWS
    blk_start = jnp.cumsum(nblk) - nblk
    pos = blk_start[flat_e] * MOE_ROWS + rank
    nsb = (nblk + SB_BLOCKS - 1) // SB_BLOCKS
    sb_end = jnp.cumsum(nsb)
    s = jnp.arange(n_sb, dtype=jnp.int32)
    total = sb_end[-1]
    sc = jnp.minimum(s, total - 1)
    e = jnp.sum((sb_end[None, :] <= sc[:, None]).astype(jnp.int32), axis=1)
    k = sc - (sb_end[e] - nsb[e])
    sb_blk0 = (blk_start[e] + k * SB_BLOCKS).astype(jnp.int32)
    sb_nblk = jnp.where(s < total, jnp.minimum(SB_BLOCKS, nblk[e] - k * SB_BLOCKS), 0).astype(jnp.int32)
    n_used = jnp.sum(nblk).astype(jnp.int32).reshape(1)
    plan = (e, sb_blk0, sb_nblk, n_used)
    return pos.reshape(T, K).astype(jnp.int32), plan, (blk_start.astype(jnp.int32), counts.astype(jnp.int32))


def _dispatch_kernel(start_ref, cnt_ref, n_used_ref, pos_ref, h_ref, xs_hbm, ring, zeros, sem, zsem):
    i = pl.program_id(0)
    n = pl.num_programs(0)
    tt = h_ref.shape[0]
    slot = i & 1

    ring[slot] = h_ref[...]

    def issue(t, c):
        for k in range(TOP_K):
            p = pos_ref[0, 0, t * TOP_K + k]
            pltpu.make_async_copy(ring.at[slot, pl.ds(t, 1)], xs_hbm.at[pl.ds(p, 1)],
                                  sem.at[slot]).start(priority=k % 2)
        return c

    lax.fori_loop(0, tt, issue, 0, unroll=2)

    def drain(sl):
        for _ in range(TOP_K):
            pltpu.make_async_copy(ring.at[sl], ring.at[sl], sem.at[sl]).wait()

    @pl.when(i == 0)
    def _():
        zeros[...] = jnp.zeros(zeros.shape, F32)

        def pad_copy(row):
            return pltpu.make_async_copy(zeros.at[pl.ds(0, 1)], xs_hbm.at[pl.ds(row, 1)], zsem)

        def tail_copy(b):
            dst = xs_hbm.at[pl.ds(pl.multiple_of(b * MOE_ROWS, MOE_ROWS), MOE_ROWS)]
            return pltpu.make_async_copy(zeros, dst, zsem)

        def per_expert(fn):
            def body(e, c):
                cnt = cnt_ref[e]
                first = start_ref[e] * MOE_ROWS + cnt
                lax.fori_loop(0, (-cnt) & (MOE_ROWS - 1), lambda j, cc: (fn(first + j), cc)[1], 0)
                return c
            lax.fori_loop(0, cnt_ref.shape[0], body, 0)

        def per_tail(fn):
            lax.fori_loop(n_used_ref[0], xs_hbm.shape[0] // MOE_ROWS, lambda b, cc: (fn(b), cc)[1], 0)

        per_expert(lambda row: pad_copy(row).start())
        per_tail(lambda b: tail_copy(b).start())
        per_expert(lambda row: pad_copy(row).wait())
        per_tail(lambda b: tail_copy(b).wait())

    @pl.when(i > 0)
    def _():
        drain(1 - slot)

    @pl.when(i == n - 1)
    def _():
        drain(slot)


def _dispatch(h, pos, pad_info, n_used, n_slots):
    T, D = h.shape
    blk_start, counts = pad_info
    tt = 256
    grid_spec = pltpu.PrefetchScalarGridSpec(
        num_scalar_prefetch=3,
        grid=(T // tt,),
        in_specs=[
            pl.BlockSpec((1, 1, tt * TOP_K), lambda i, s, c, u: (i, 0, 0), memory_space=pltpu.SMEM),
            pl.BlockSpec((tt, D), lambda i, s, c, u: (i, 0)),
        ],
        out_specs=pl.BlockSpec(memory_space=pl.ANY),
        scratch_shapes=[pltpu.VMEM((2, tt, D), F32), pltpu.VMEM((MOE_ROWS, D), F32),
                        pltpu.SemaphoreType.DMA((2,)), pltpu.SemaphoreType.DMA],
    )
    return pl.pallas_call(
        _dispatch_kernel,
        out_shape=jax.ShapeDtypeStruct((n_slots, D), F32),
        grid_spec=grid_spec,
        compiler_params=_cparams(("arbitrary",)),
        name="moe_dispatch",
    )(blk_start, counts, n_used, pos.reshape(T // tt, 1, tt * TOP_K), h)


def _moe_kernel(sb_e_ref, sb_blk0_ref, sb_nblk_ref, n_used_ref, xs_hbm, wg_ref, bg_ref, wu_ref, bu_ref,
                wd_ref, bd_ref, y_hbm, x_s, y_s, stage, sem_in, sem_out):
    s = pl.program_id(0)
    f = pl.program_id(1)
    n_s = pl.num_programs(0)
    n_f = pl.num_programs(1)
    nblk = sb_nblk_ref[s]
    blk0 = sb_blk0_ref[s]
    tf = wg_ref.shape[3]
    n_stage = stage.shape[0]

    def rows_of(blk, nb):
        return pl.ds(pl.multiple_of(blk * MOE_ROWS, MOE_ROWS), nb * MOE_ROWS)

    def in_copy(first_blk, r):
        return pltpu.make_async_copy(xs_hbm.at[rows_of(first_blk + r, 1)], stage.at[r & (n_stage - 1)],
                                     sem_in.at[r & (n_stage - 1)])

    def out_copy(first_blk, r):
        return pltpu.make_async_copy(y_s.at[rows_of(r, 1)], y_hbm.at[rows_of(first_blk + r, 1)], sem_out)

    def for_blocks(n, fn):
        def body(r, c):
            fn(r)
            return c
        lax.fori_loop(0, n, body, 0)

    def prefetch(first_blk, n):
        for_blocks(jnp.minimum(n, n_stage), lambda r: in_copy(first_blk, r).start())

    def ffn(blk, nb, first):
        rows = rows_of(blk, nb)
        if first:
            for i in range(nb):
                r = blk + i
                in_copy(blk0, r).wait()
                x_s[rows_of(r, 1), :] = stage[r & (n_stage - 1)].astype(BF16)

                @pl.when(r + n_stage < nblk)
                def _(r=r):
                    in_copy(blk0, r + n_stage).start()

        wgu = jnp.concatenate([wg_ref[0, 0].astype(BF16), wu_ref[0, 0].astype(BF16)], axis=1)
        gu = _dot(x_s[rows, :], wgu)
        g = jnp.minimum(gu[:, :tf] + bg_ref[0, 0], SWIGLU_LIMIT)
        u = jnp.clip(gu[:, tf:] + bu_ref[0, 0], -SWIGLU_LIMIT, SWIGLU_LIMIT)
        act = (u + 1.0) * g * jax.nn.sigmoid(SWIGLU_ALPHA * g)
        y = _dot(act.astype(BF16), wd_ref[0, 0].astype(BF16))
        if first:
            y_s[rows, :] = y + bd_ref[0, 0]
        else:
            y_s[rows, :] += y

    def all_chunks(first):
        n_big = lax.shift_right_logical(nblk, FFN_CHUNK_LOG2)
        for_blocks(n_big, lambda c: ffn(c * FFN_CHUNK, FFN_CHUNK, first))
        done = n_big * FFN_CHUNK
        nb = FFN_CHUNK // 2
        while nb >= 1:
            @pl.when((nblk & nb) != 0)
            def _(done=done, nb=nb):
                ffn(done, nb, first)
            done = done + (nblk & nb)
            nb //= 2

    @pl.when(nblk > 0)
    def _():
        @pl.when(f == 0)
        def _():
            @pl.when(s == 0)
            def _():
                prefetch(blk0, nblk)

            @pl.when(s > 0)
            def _():
                for_blocks(sb_nblk_ref[s - 1], lambda r: out_copy(sb_blk0_ref[s - 1], r).wait())

            all_chunks(True)

        @pl.when(f > 0)
        def _():
            all_chunks(False)

        @pl.when(f == n_f - 1)
        def _():
            for_blocks(nblk, lambda r: out_copy(blk0, r).start())
            nxt = jnp.minimum(s + 1, n_s - 1)
            has_next = jnp.logical_and(s + 1 < n_s, sb_nblk_ref[nxt] > 0)

            @pl.when(has_next)
            def _():
                prefetch(sb_blk0_ref[nxt], sb_nblk_ref[nxt])

            @pl.when(jnp.logical_not(has_next))
            def _():
                for_blocks(nblk, lambda r: out_copy(blk0, r).wait())

    @pl.when(jnp.logical_and(s == pl.num_programs(0) - 1, f == n_f - 1))
    def _():
        y_s[0:MOE_ROWS, :] = jnp.zeros((MOE_ROWS, y_s.shape[1]), F32)

        def tail_copy(r):
            dst = y_hbm.at[pl.ds(pl.multiple_of(r * MOE_ROWS, MOE_ROWS), MOE_ROWS)]
            return pltpu.make_async_copy(y_s.at[0:MOE_ROWS], dst, sem_out)

        def start(r, c):
            tail_copy(r).start()
            return c

        def wait(r, c):
            tail_copy(r).wait()
            return c

        lax.fori_loop(n_used_ref[0], y_hbm.shape[0] // MOE_ROWS, start, 0)
        lax.fori_loop(n_used_ref[0], y_hbm.shape[0] // MOE_ROWS, wait, 0)


def _moe_ffn(xs, plan, w_gate, b_gate, w_up, b_up, w_down, b_down, l, n_sb):
    sb_e, sb_blk0, sb_nblk, n_used = plan
    n_slots, D = xs.shape
    L, E, _, F = w_gate.shape
    tf = 256
    rows = SB_BLOCKS * MOE_ROWS
    n_f = F // tf

    def ftile(s, f, n):
        return jnp.where(n[s] > 0, f, n_f - 1)

    grid_spec = pltpu.PrefetchScalarGridSpec(
        num_scalar_prefetch=4,
        grid=(n_sb, n_f),
        in_specs=[
            pl.BlockSpec(memory_space=pl.ANY),
            pl.BlockSpec((1, 1, D, tf), lambda s, f, e, b, n, u: (l, e[s], 0, ftile(s, f, n))),
            pl.BlockSpec((1, 1, 1, tf), lambda s, f, e, b, n, u: (l, e[s], 0, ftile(s, f, n))),
            pl.BlockSpec((1, 1, D, tf), lambda s, f, e, b, n, u: (l, e[s], 0, ftile(s, f, n))),
            pl.BlockSpec((1, 1, 1, tf), lambda s, f, e, b, n, u: (l, e[s], 0, ftile(s, f, n))),
            pl.BlockSpec((1, 1, tf, D), lambda s, f, e, b, n, u: (l, e[s], ftile(s, f, n), 0)),
            pl.BlockSpec((1, 1, 1, D), lambda s, f, e, b, n, u: (l, e[s], 0, 0)),
        ],
        out_specs=pl.BlockSpec(memory_space=pl.ANY),
        scratch_shapes=[
            pltpu.VMEM((rows, D), BF16),
            pltpu.VMEM((rows, D), F32),
            pltpu.VMEM((XS_STAGES, MOE_ROWS, D), F32),
            pltpu.SemaphoreType.DMA((XS_STAGES,)),
            pltpu.SemaphoreType.DMA,
        ],
    )
    return pl.pallas_call(
        _moe_kernel,
        out_shape=jax.ShapeDtypeStruct((n_slots, D), F32),
        grid_spec=grid_spec,
        compiler_params=_cparams(("arbitrary", "arbitrary")),
        name="moe_ffn",
    )(sb_e, sb_blk0, sb_nblk, n_used, xs, w_gate, b_gate.reshape(L, E, 1, F), w_up, b_up.reshape(L, E, 1, F),
      w_down, b_down.reshape(L, E, 1, D))


def _combine_kernel(pos_ref, pos_next_ref, y_hbm, gates_ref, x_ref, mods_ref, g_ref, o_ref, stage, sem, *,
                    final_norm):
    i = pl.program_id(0)
    n_steps = pl.num_programs(0)
    n = x_ref.shape[0]
    slot = i & 1

    def fetch(p_ref, sl):
        def issue(t, c):
            for k in range(TOP_K):
                p = p_ref[0, 0, t * TOP_K + k]
                pltpu.make_async_copy(y_hbm.at[pl.ds(p, 1)], stage.at[sl, k, pl.ds(t, 1)],
                                      sem.at[sl]).start(priority=k % 2)
            return c
        lax.fori_loop(0, n, issue, 0, unroll=2)

    @pl.when(i == 0)
    def _():
        fetch(pos_ref, slot)

    @pl.when(i + 1 < n_steps)
    def _():
        fetch(pos_next_ref, 1 - slot)

    for k in range(TOP_K):
        pltpu.make_async_copy(stage.at[slot, k], stage.at[slot, k], sem.at[slot]).wait()
    gates = gates_ref[...]
    acc = gates[:, 0:1] * stage[slot, 0]
    for k in range(1, TOP_K):
        acc = acc + gates[:, k:k + 1] * stage[slot, k]
    res = x_ref[...] + mods_ref[0][5:6] * acc
    o_ref[...] = _rms_lanes(res, g_ref[...]) if final_norm else res


def _combine(y_slots, pos, gates, x, mods, g_final, final_norm, n_ctx_rows, dec_seq):
    T, D = x.shape
    cb = 128
    n_steps = T // cb
    return pl.pallas_call(
        functools.partial(_combine_kernel, final_norm=final_norm),
        out_shape=jax.ShapeDtypeStruct((T, D), F32),
        grid=(n_steps,),
        in_specs=[
            pl.BlockSpec((1, 1, cb * TOP_K), lambda i: (i, 0, 0), memory_space=pltpu.SMEM),
            pl.BlockSpec((1, 1, cb * TOP_K), lambda i: (jnp.minimum(i + 1, n_steps - 1), 0, 0),
                         memory_space=pltpu.SMEM),
            pl.BlockSpec(memory_space=pl.ANY),
            pl.BlockSpec((cb, LANES), lambda i: (i, 0)),
            pl.BlockSpec((cb, D), lambda i: (i, 0)),
            pl.BlockSpec((1, N_MOD, D), lambda i: (_cond_index(i, cb, n_ctx_rows, dec_seq), 0, 0)),
            pl.BlockSpec((1, D), lambda i: (0, 0)),
        ],
        out_specs=pl.BlockSpec((cb, D), lambda i: (i, 0)),
        scratch_shapes=[pltpu.VMEM((2, TOP_K, cb, D), F32), pltpu.SemaphoreType.DMA((2,))],
        compiler_params=_cparams(("arbitrary",)),
        name="moe_combine",
    )(pos.reshape(n_steps, 1, cb * TOP_K), pos.reshape(n_steps, 1, cb * TOP_K), y_slots, gates, x, mods, g_final)


def _rope_tables(n, head_dim):
    quarter = head_dim // 4
    inv = ROPE_THETA ** (-jnp.arange(quarter, dtype=F32) / quarter)
    pos = jnp.arange(n)
    row = (pos // GRID_W).astype(F32)[:, None] * inv
    col = (pos % GRID_W).astype(F32)[:, None] * inv
    cos = jnp.concatenate([jnp.cos(row), jnp.cos(row), jnp.cos(col), jnp.cos(col)], axis=1)
    sin = jnp.concatenate([-jnp.sin(row), jnp.sin(row), -jnp.sin(col), jnp.sin(col)], axis=1)
    reps = LANES // head_dim
    return jnp.tile(cos, (1, reps)), jnp.tile(sin, (1, reps))


def kernel(x_prompt, x_sample, c, cache_k_a, cache_v_a, cache_k_c, cache_v_c, c_ctx, w_mod, b_mod, g_norm1, g_norm2, w_in, lam_q1, lam_k1, lam_q2, lam_k2, g_subln_a, g_q_c, g_k_c, g_v_b, w_s_b, b_s_b, w_d, s_d, w_out, w_router, b_router, w_gate, b_gate, w_up, b_up, w_down, b_down, g_final):
    B, S, D = x_prompt.shape
    DB, DS, _ = x_sample.shape
    L = w_mod.shape[0]
    P = cache_k_a.shape[2]
    E = w_router.shape[2]
    n_ctx = B * S
    T = n_ctx + DB * DS
    GW = w_out.shape[1] // 4
    dk_a = cache_k_a.shape[-1]
    hd_c = cache_k_c.shape[-1]

    x = jnp.concatenate([x_prompt.reshape(n_ctx, D), x_sample.reshape(DB * DS, D)], axis=0)
    cond8 = jnp.zeros((8, D), F32).at[0].set(c_ctx).at[1:1 + DB].set(c)
    mod = _modulation(cond8, w_mod, b_mod).reshape(L, 8, N_MOD, D)

    cos_a, sin_a = _rope_tables(DS, dk_a)
    cos_c, sin_c = _rope_tables(DS, hd_c)
    caches_all = (cache_k_a.reshape(DB, L, P, GW), cache_v_a.reshape(DB, L, P, GW),
                  cache_k_c.reshape(DB, L, P, GW // 2), cache_v_c.reshape(DB, L, P, GW // 2))

    w_in = _to_bf16(w_in)
    w_out = _to_bf16(w_out)
    n_slots_max = -(-(T * TOP_K + E * (MOE_ROWS - 1)) // MOE_ROWS)
    n_sb = E + n_slots_max // SB_BLOCKS

    states = []
    for l in range(L):
        mods = mod[l]
        lam0 = 0.8 - 0.6 * math.exp(-0.3 * l)
        proj = _inproj(x, g_norm1[l][None, :], mods, w_in, l, n_ctx, DS)
        params = (jnp.stack([lam_q1[l], lam_k1[l], lam_q2[l], lam_k2[l]]), g_subln_a[l][None, :],
                  g_q_c[l][None, :], g_k_c[l][None, :], g_v_b[l][None, :], w_s_b[l], b_s_b[l].T,
                  w_d[l], s_d[l][None, :])
        mix_c, kc_norm = _mixer(proj, l, 0, B, S, lam0, params)
        mix_s = _mixer(proj, l, n_ctx, DB, DS, lam0, params, caches=caches_all,
                       rope_tabs=(cos_a, sin_a, cos_c, sin_c))
        pc = proj[:n_ctx]
        states.append((pc[:, GW:2 * GW], pc[:, 2 * GW:3 * GW], kc_norm, pc[:, 6 * GW + GW // 2:7 * GW]))
        x = _outproj(mix_c, mix_s, w_out, x, mods, l, n_ctx, DS)
        h, top_idx, gates = _router(x, g_norm2[l][None, :], mods, w_router[l], b_router[l], n_ctx, DS)
        pos, plan, pad_info = _route_plan(top_idx[:, :TOP_K], E, n_sb)
        xs = _dispatch(h, pos, pad_info, plan[3], n_slots_max * MOE_ROWS)
        y_slots = _moe_ffn(xs, plan, w_gate, b_gate, w_up, b_up, w_down, b_down, l, n_sb)
        x = _combine(y_slots, pos, gates, x, mods, g_final[None, :], l == L - 1, n_ctx, DS)

    y = x
    y_prompt = y[:n_ctx].reshape(B, S, D)
    y_sample = y[n_ctx:].reshape(DB, DS, D)
    h_a = GW // (2 * dk_a)
    new_k_a = jnp.stack([st[0].reshape(B, S, h_a, 2, dk_a) for st in states], axis=1)
    new_v_a = jnp.stack([st[1].reshape(B, S, h_a, 2 * dk_a) for st in states], axis=1)
    new_k_c = jnp.stack([st[2].reshape(B, S, GW // 2 // hd_c, hd_c) for st in states], axis=1)
    new_v_c = jnp.stack([st[3].reshape(B, S, GW // 2 // hd_c, hd_c) for st in states], axis=1)
    return (y_prompt, y_sample, new_k_a, new_v_a, new_k_c, new_v_c)
```

```python
import functools
import math

import jax
import jax.numpy as jnp
import numpy as np
from jax import lax
from jax.experimental import pallas as pl
from jax.experimental.pallas import tpu as pltpu

F32 = jnp.float32
BF16 = jnp.bfloat16

EPS = 1e-6
GRID_W = 64
ROPE_THETA = 10000.0
N_MOD = 6
TOP_K = 4
SWIGLU_LIMIT = 7.0
SWIGLU_ALPHA = 1.702
POOL_WINDOWS = (2, 4, 8, 16)

LANES = 128
CHUNK = 128
MOE_ROWS = 128
SB_BLOCKS = 16
XS_STAGES = 8
FFN_CHUNK_LOG2 = 3
FFN_CHUNK = 1 << FFN_CHUNK_LOG2
VMEM_LIMIT = 56 * 1024 * 1024


def _cparams(sem, vmem=VMEM_LIMIT):
    return pltpu.CompilerParams(dimension_semantics=sem, vmem_limit_bytes=vmem)


def _dot(a, b):
    return jnp.dot(a, b, preferred_element_type=F32)


def _dot_nt(a, b):
    return lax.dot_general(a, b, (((1,), (1,)), ((), ())), preferred_element_type=F32)


def _rms_lanes(x, g):
    ms = jnp.mean(x * x, axis=-1, keepdims=True)
    return x * lax.rsqrt(ms + EPS) * g


def _mod_kernel(cond_ref, w_ref, b_ref, o_ref):
    c = cond_ref[...]
    a = (c * jax.nn.sigmoid(c)).astype(BF16)
    o_ref[0] = _dot(a, w_ref[0].astype(BF16)) + b_ref[0]


def _modulation(cond8, w_mod, b_mod):
    L, D, NM = w_mod.shape
    tn = 1024
    return pl.pallas_call(
        _mod_kernel,
        out_shape=jax.ShapeDtypeStruct((L, 8, NM), F32),
        grid=(L, NM // tn),
        in_specs=[
            pl.BlockSpec((8, D), lambda l, n: (0, 0)),
            pl.BlockSpec((1, D, tn), lambda l, n: (l, 0, n)),
            pl.BlockSpec((1, 1, tn), lambda l, n: (l, 0, n)),
        ],
        out_specs=pl.BlockSpec((1, 8, tn), lambda l, n: (l, 0, n)),
        compiler_params=_cparams(("arbitrary", "arbitrary")),
        name="modulation",
    )(cond8, w_mod, b_mod.reshape(L, 1, NM))


def _cast_kernel(w_ref, o_ref):
    o_ref[...] = w_ref[...].astype(BF16)


def _to_bf16(w):
    L, K, N = w.shape
    tn = 512
    return pl.pallas_call(
        _cast_kernel,
        out_shape=jax.ShapeDtypeStruct(w.shape, BF16),
        grid=(L, N // tn),
        in_specs=[pl.BlockSpec((1, K, tn), lambda l, n: (l, 0, n))],
        out_specs=pl.BlockSpec((1, K, tn), lambda l, n: (l, 0, n)),
        compiler_params=_cparams(("arbitrary", "arbitrary")),
        name="weights_to_bf16",
    )(w)


def _cond_index(row_block, rows_per_block, n_ctx_rows, dec_seq):
    r0 = row_block * rows_per_block
    return jnp.where(r0 < n_ctx_rows, 0, 1 + (r0 - n_ctx_rows) // dec_seq)


def _inproj_kernel(x_ref, g_ref, mods_ref, w_ref, o_ref, h_ref, *, shift_row, scale_row, chunk):
    @pl.when(pl.program_id(1) == 0)
    def _():
        m = mods_ref[0]
        shift = m[shift_row:shift_row + 1]
        scale1 = 1.0 + m[scale_row:scale_row + 1]
        g = g_ref[...]

        def body(i, carry):
            r = pl.multiple_of(i * chunk, chunk)
            x = x_ref[pl.ds(r, chunk), :]
            h_ref[pl.ds(r, chunk), :] = (_rms_lanes(x, g) * scale1 + shift).astype(BF16)
            return carry

        lax.fori_loop(0, x_ref.shape[0] // chunk, body, 0)

    o_ref[...] = _dot(h_ref[...], w_ref[0])


def _inproj(x, g, mods, w_in, l, n_ctx_rows, dec_seq):
    T, D = x.shape
    PW = w_in.shape[2]
    tm, tn = 1024, 1024
    kern = functools.partial(_inproj_kernel, shift_row=0, scale_row=1, chunk=128)
    return pl.pallas_call(
        kern,
        out_shape=jax.ShapeDtypeStruct((T, PW), F32),
        grid=(T // tm, PW // tn),
        in_specs=[
            pl.BlockSpec((tm, D), lambda m, n: (m, 0)),
            pl.BlockSpec((1, D), lambda m, n: (0, 0)),
            pl.BlockSpec((1, N_MOD, D), lambda m, n: (_cond_index(m, tm, n_ctx_rows, dec_seq), 0, 0)),
            pl.BlockSpec((1, D, tn), lambda m, n: (l, 0, n)),
        ],
        out_specs=pl.BlockSpec((tm, tn), lambda m, n: (m, n)),
        scratch_shapes=[pltpu.VMEM((tm, D), BF16)],
        compiler_params=_cparams(("arbitrary", "arbitrary")),
        name="inproj",
    )(x, g, mods, w_in)


def _rope(x, cos, sin, q):
    lane = lax.broadcasted_iota(jnp.int32, x.shape, 1)
    first_half = (lane & q) == 0
    partner = jnp.where(first_half, pltpu.roll(x, LANES - q, 1), pltpu.roll(x, q, 1))
    return x * cos + partner * sin


def _exp_scores(s):
    return jnp.exp(s - jnp.max(s, axis=-1, keepdims=True)).astype(BF16)


def _mixer_kernel(*refs, seq, qb, n_cache, lam0, rope):
    it = iter(refs)
    qa_ref, ub_ref, vb_ref, qc_ref = next(it), next(it), next(it), next(it)
    ka_ref, va_ref, kv_ref, pd_ref = next(it), next(it), next(it), next(it)
    if n_cache:
        cka_ref, cva_ref, ckc_ref, cvc_ref = next(it), next(it), next(it), next(it)
    if rope:
        cos_a_ref, sin_a_ref, cos_c_ref, sin_c_ref = next(it), next(it), next(it), next(it)
    lamp_ref, gsub_ref, gq_ref, gk_ref, gvb_ref = next(it), next(it), next(it), next(it), next(it)
    ws_ref, bst_ref, wd_ref, sd_ref = next(it), next(it), next(it), next(it)
    mix_ref = next(it)
    kcn_ref = None if n_cache else next(it)
    ka_s, va_s, kc_s, vc_s = next(it), next(it), next(it), next(it)

    j = pl.program_id(1)
    row0 = pl.multiple_of(j * qb, qb)
    n_heads = qa_ref.shape[1] // LANES
    n_kv = kc_s.shape[1] // LANES
    dk_a = LANES // 2

    @pl.when(j == 0)
    def _():
        prep = 256

        def put_values(dst, rows, b, v):
            dst[rows, 2 * b * LANES:(2 * b + 1) * LANES] = v.astype(BF16)
            dst[rows, (2 * b + 1) * LANES:(2 * b + 2) * LANES] = jnp.ones(v.shape, BF16)

        for c in range(seq // prep):
            rows = slice(c * prep, (c + 1) * prep)
            for b in range(n_heads):
                cols = slice(b * LANES, (b + 1) * LANES)
                k = ka_ref[rows, cols]
                if rope:
                    k = _rope(k, cos_a_ref[rows, :], sin_a_ref[rows, :], dk_a // 4)
                ka_s[rows, cols] = k.astype(BF16)
                put_values(va_s, rows, b, va_ref[rows, cols])
            for b in range(n_kv):
                cols = slice(b * LANES, (b + 1) * LANES)
                k = _rms_lanes(kv_ref[rows, cols], gk_ref[...])
                if kcn_ref is not None:
                    kcn_ref[rows, cols] = k
                if rope:
                    k = _rope(k, cos_c_ref[rows, :], sin_c_ref[rows, :], LANES // 4)
                kc_s[rows, cols] = k.astype(BF16)
                put_values(vc_s, rows, b, kv_ref[rows, (n_kv + b) * LANES:(n_kv + b + 1) * LANES])
        if n_cache:
            tail = slice(seq, seq + n_cache)
            ka_s[tail, :] = cka_ref[0].astype(BF16)
            kc_s[tail, :] = ckc_ref[0].astype(BF16)
            for b in range(n_heads):
                put_values(va_s, tail, b, cva_ref[0, :, b * LANES:(b + 1) * LANES])
            for b in range(n_kv):
                put_values(vc_s, tail, b, cvc_ref[0, :, b * LANES:(b + 1) * LANES])

    lane = lax.broadcasted_iota(jnp.int32, (qb, LANES), 1)

    lp = lamp_ref[...]
    lam = (jnp.exp(jnp.sum(lp[0:1] * lp[1:2], axis=-1, keepdims=True))
           - jnp.exp(jnp.sum(lp[2:3] * lp[3:4], axis=-1, keepdims=True)) + lam0)
    scale_a = dk_a ** -0.5
    if rope:
        cos_a = cos_a_ref[pl.ds(row0, qb), :]
        sin_a = sin_a_ref[pl.ds(row0, qb), :]
        cos_c = cos_c_ref[pl.ds(row0, qb), :]
        sin_c = sin_c_ref[pl.ds(row0, qb), :]
    for h in range(n_heads):
        cols = slice(h * LANES, (h + 1) * LANES)
        q = qa_ref[:, cols]
        if rope:
            q = _rope(q, cos_a, sin_a, dk_a // 4)
        q = q * scale_a
        k = ka_s[:, cols]
        v1 = va_s[:, 2 * h * LANES:(2 * h + 2) * LANES]
        q0 = jnp.where(lane < dk_a, q, 0.0).astype(BF16)
        q1 = jnp.where(lane >= dk_a, q, 0.0).astype(BF16)
        o0 = _dot(_exp_scores(_dot_nt(q0, k)), v1)
        o1 = _dot(_exp_scores(_dot_nt(q1, k)), v1)
        o = o0[:, :LANES] * (1.0 / o0[:, LANES:LANES + 1]) - o1[:, :LANES] * (lam / o1[:, LANES:LANES + 1])
        mix_ref[:, cols] = (_rms_lanes(o, gsub_ref[...]) * (1.0 - lam0)).astype(BF16)

    off_b = n_heads * LANES
    for g in range(ub_ref.shape[1] // LANES):
        cols = slice(g * LANES, (g + 1) * LANES)
        vn = _rms_lanes(vb_ref[:, cols], gvb_ref[:, cols]).astype(BF16)
        w_s = ws_ref[g].astype(BF16)
        for c in range(qb // CHUNK):
            rows = slice(c * CHUNK, (c + 1) * CHUNK)
            mixed = _dot(w_s, vn[rows, :]) + bst_ref[:, g:g + 1]
            mix_ref[rows, off_b + g * LANES: off_b + (g + 1) * LANES] = (ub_ref[rows, cols] * mixed).astype(BF16)

    off_c = off_b + ub_ref.shape[1]
    scale_c = LANES ** -0.5
    for h in range(n_heads):
        cols = slice(h * LANES, (h + 1) * LANES)
        q = _rms_lanes(qc_ref[:, cols], gq_ref[...])
        if rope:
            q = _rope(q, cos_c, sin_c, LANES // 4)
        kv = h // (n_heads // n_kv)
        e = _exp_scores(_dot_nt((q * scale_c).astype(BF16), kc_s[:, kv * LANES:(kv + 1) * LANES]))
        o = _dot(e, vc_s[:, 2 * kv * LANES:(2 * kv + 2) * LANES])
        mix_ref[:, off_c + h * LANES: off_c + (h + 1) * LANES] = (
            o[:, :LANES] * (1.0 / o[:, LANES:LANES + 1])).astype(BF16)

    off_d = off_c + qc_ref.shape[1]
    win = min(qb + 2 * CHUNK, seq)
    start = pl.multiple_of(jnp.clip(row0 - CHUNK, 0, seq - win), CHUNK)
    t = row0 + lax.broadcasted_iota(jnp.int32, (qb, win), 0)
    col = start + lax.broadcasted_iota(jnp.int32, (qb, win), 1)
    t1 = row0 + lax.broadcasted_iota(jnp.int32, (qb, 1), 0)
    for g, w in enumerate(POOL_WINDOWS):
        cols = slice(g * LANES, (g + 1) * LANES)
        left, right = w // 2, w - 1 - w // 2
        band = jnp.where((col >= t - left) & (col <= t + right), 1.0, 0.0).astype(BF16)
        cnt = (jnp.minimum(t1 + right + 1, seq) - jnp.maximum(t1 - left, 0)).astype(F32)
        xw = pd_ref[pl.ds(start, win), cols]
        hi = xw.astype(BF16)
        lo = (xw - hi.astype(F32)).astype(BF16)
        wsum = _dot(band, hi) + _dot(band, lo)
        xg = pd_ref[pl.ds(row0, qb), cols]
        pooled = wsum / cnt - xg
        y = _dot(pooled.astype(BF16), wd_ref[g].astype(BF16)) * sd_ref[:, cols]
        mix_ref[:, off_d + g * LANES: off_d + (g + 1) * LANES] = y.astype(BF16)


def _mixer(proj, l, row_start, n_seq, seq, qb, lam0, params, caches=None, rope_tabs=None):
    T, PW = proj.shape
    GW = PW // 8
    n_qb = seq // qb
    qb0 = row_start // qb
    sb0 = row_start // seq
    n_cache = 0 if caches is None else caches[0].shape[2]
    n_keys = seq + n_cache

    def qspec(col):
        return pl.BlockSpec((qb, GW), lambda s, j: (qb0 + s * n_qb + j, col))

    def sspec(col):
        return pl.BlockSpec((seq, GW), lambda s, j: (sb0 + s, col))

    def full(shape):
        nd = len(shape)
        return pl.BlockSpec(shape, lambda s, j: (0,) * nd)

    in_specs = [qspec(0), qspec(3), qspec(4), qspec(5), sspec(1), sspec(2), sspec(6), sspec(7)]
    args = [proj] * 8
    if caches is not None:
        for c in caches:
            in_specs.append(pl.BlockSpec((None, 1, c.shape[2], c.shape[3]), lambda s, j: (s, l, 0, 0)))
            args.append(c)
    if rope_tabs is not None:
        for tab in rope_tabs:
            in_specs.append(full(tab.shape))
            args.append(tab)
    for p in params:
        in_specs.append(full(p.shape))
        args.append(p)

    MW = 4 * GW
    mix_spec = pl.BlockSpec((qb, MW), lambda s, j: (s * n_qb + j, 0))
    if caches is None:
        out_shape = (jax.ShapeDtypeStruct((n_seq * seq, MW), BF16),
                     jax.ShapeDtypeStruct((n_seq * seq, GW // 2), F32))
        out_specs = (mix_spec, pl.BlockSpec((seq, GW // 2), lambda s, j: (s, 0)))
    else:
        out_shape = jax.ShapeDtypeStruct((n_seq * seq, MW), BF16)
        out_specs = mix_spec

    kern = functools.partial(_mixer_kernel, seq=seq, qb=qb, n_cache=n_cache, lam0=lam0,
                             rope=rope_tabs is not None)
    return pl.pallas_call(
        kern,
        out_shape=out_shape,
        grid=(n_seq, n_qb),
        in_specs=in_specs,
        out_specs=out_specs,
        scratch_shapes=[pltpu.VMEM((n_keys, GW), BF16), pltpu.VMEM((n_keys, 2 * GW), BF16),
                        pltpu.VMEM((n_keys, GW // 2), BF16), pltpu.VMEM((n_keys, GW), BF16)],
        compiler_params=_cparams(("arbitrary", "arbitrary")),
        name="mixer_latent" if caches is not None else "mixer_context",
    )(*args)


def _outproj_kernel(mix_c_ref, mix_s_ref, w_ref, x_ref, mods_ref, o_ref, *, gate_row, n_ctx_tiles):
    tn = o_ref.shape[1]
    gcols = pl.ds(pl.multiple_of(pl.program_id(1) * tn, tn), tn)
    gate = mods_ref[0, gate_row:gate_row + 1, gcols]
    w = w_ref[0]
    is_ctx = pl.program_id(0) < n_ctx_tiles

    @pl.when(is_ctx)
    def _():
        o_ref[...] = x_ref[...] + gate * _dot(mix_c_ref[...], w)

    @pl.when(jnp.logical_not(is_ctx))
    def _():
        o_ref[...] = x_ref[...] + gate * _dot(mix_s_ref[...], w)


def _outproj(mix_c, mix_s, w_out, x, mods, l, n_ctx_rows, dec_seq):
    T, D = x.shape
    MW = mix_c.shape[1]
    tm, tn = 1024, 1024
    nct = n_ctx_rows // tm
    kern = functools.partial(_outproj_kernel, gate_row=2, n_ctx_tiles=nct)
    return pl.pallas_call(
        kern,
        out_shape=jax.ShapeDtypeStruct((T, D), F32),
        grid=(T // tm, D // tn),
        in_specs=[
            pl.BlockSpec((tm, MW), lambda m, n: (jnp.minimum(m, nct - 1), 0)),
            pl.BlockSpec((tm, MW), lambda m, n: (jnp.maximum(m - nct, 0), 0)),
            pl.BlockSpec((1, MW, tn), lambda m, n: (l, 0, n)),
            pl.BlockSpec((tm, tn), lambda m, n: (m, n)),
            pl.BlockSpec((1, N_MOD, D), lambda m, n: (_cond_index(m, tm, n_ctx_rows, dec_seq), 0, 0)),
        ],
        out_specs=pl.BlockSpec((tm, tn), lambda m, n: (m, n)),
        compiler_params=_cparams(("arbitrary", "arbitrary")),
        name="outproj",
    )(mix_c, mix_s, w_out, x, mods)


def _split_bf16(x):
    hi = x.astype(BF16)
    return hi, (x - hi.astype(F32)).astype(BF16)


def _router_kernel(x_ref, g_ref, mods_ref, wr_ref, br_ref, h_ref, idx_ref, gate_ref, *, n_experts):
    m = mods_ref[0]
    h = _rms_lanes(x_ref[...], g_ref[...]) * (1.0 + m[4:5]) + m[3:4]
    h_ref[...] = h
    h_hi, h_lo = _split_bf16(h)
    w_hi, w_lo = _split_bf16(wr_ref[...])
    logits = _dot(h_hi, w_hi) + _dot(h_hi, w_lo) + _dot(h_lo, w_hi) + br_ref[...]
    lane = lax.broadcasted_iota(jnp.int32, logits.shape, 1)
    lane_f = lane.astype(F32)
    neg = jnp.float32(-jnp.inf)
    logits = jnp.where(lane < n_experts, logits, neg)
    idx_out = jnp.zeros(logits.shape, F32)
    val_out = jnp.zeros(logits.shape, F32)
    top = None
    denom = None
    for k in range(TOP_K):
        v = jnp.max(logits, axis=-1, keepdims=True)
        i = jnp.min(jnp.where(logits == v, lane_f, float(LANES)), axis=-1, keepdims=True)
        if k == 0:
            top = v
        e = jnp.exp(v - top)
        denom = e if denom is None else denom + e
        idx_out = jnp.where(lane == k, i, idx_out)
        val_out = jnp.where(lane == k, e, val_out)
        logits = jnp.where(lane_f == i, neg, logits)
    idx_ref[...] = idx_out.astype(jnp.int32)
    gate_ref[...] = val_out / denom


def _router(x, g, mods, w_router_l, b_router_l, n_ctx_rows, dec_seq):
    T, D = x.shape
    E = w_router_l.shape[1]
    tm = 512
    wr = jnp.zeros((D, LANES), F32).at[:, :E].set(w_router_l)
    br = jnp.zeros((1, LANES), F32).at[0, :E].set(b_router_l)
    kern = functools.partial(_router_kernel, n_experts=E)
    return pl.pallas_call(
        kern,
        out_shape=(jax.ShapeDtypeStruct((T, D), F32),
                   jax.ShapeDtypeStruct((T, LANES), jnp.int32),
                   jax.ShapeDtypeStruct((T, LANES), F32)),
        grid=(T // tm,),
        in_specs=[
            pl.BlockSpec((tm, D), lambda m: (m, 0)),
            pl.BlockSpec((1, D), lambda m: (0, 0)),
            pl.BlockSpec((1, N_MOD, D), lambda m: (_cond_index(m, tm, n_ctx_rows, dec_seq), 0, 0)),
            pl.BlockSpec((D, LANES), lambda m: (0, 0)),
            pl.BlockSpec((1, LANES), lambda m: (0, 0)),
        ],
        out_specs=(pl.BlockSpec((tm, D), lambda m: (m, 0)),
                   pl.BlockSpec((tm, LANES), lambda m: (m, 0)),
                   pl.BlockSpec((tm, LANES), lambda m: (m, 0))),
        compiler_params=_cparams(("arbitrary",)),
        name="norm_router",
    )(x, g, mods, wr, br)


def _route_plan(top_idx, n_experts, n_sb):
    T, K = top_idx.shape
    flat_e = top_idx.reshape(-1)
    onehot = (flat_e[:, None] == jnp.arange(n_experts, dtype=jnp.int32)[None, :]).astype(jnp.int32)
    csum = jnp.cumsum(onehot, axis=0)
    counts = csum[-1]
    rank = jnp.take_along_axis(csum, flat_e[:, None], axis=1)[:, 0] - 1
    nblk = (counts + MOE_ROWS - 1) // MOE_ROWS
    blk_start = jnp.cumsum(nblk) - nblk
    pos = blk_start[flat_e] * MOE_ROWS + rank
    nsb = (nblk + SB_BLOCKS - 1) // SB_BLOCKS
    sb_end = jnp.cumsum(nsb)
    s = jnp.arange(n_sb, dtype=jnp.int32)
    total = sb_end[-1]
    sc = jnp.minimum(s, total - 1)
    e = jnp.sum((sb_end[None, :] <= sc[:, None]).astype(jnp.int32), axis=1)
    k = sc - (sb_end[e] - nsb[e])
    sb_blk0 = (blk_start[e] + k * SB_BLOCKS).astype(jnp.int32)
    sb_nblk = jnp.where(s < total, jnp.minimum(SB_BLOCKS, nblk[e] - k * SB_BLOCKS), 0).astype(jnp.int32)
    n_used = jnp.sum(nblk).astype(jnp.int32).reshape(1)
    plan = (e, sb_blk0, sb_nblk, n_used)
    return pos.reshape(T, K).astype(jnp.int32), plan, (blk_start.astype(jnp.int32), counts.astype(jnp.int32))


def _dispatch_kernel(start_ref, cnt_ref, n_used_ref, pos_ref, h_ref, xs_hbm, ring, zeros, sem, zsem):
    i = pl.program_id(0)
    n = pl.num_programs(0)
    tt = h_ref.shape[0]
    slot = i & 1

    ring[slot] = h_ref[...]

    def issue(t, c):
        for k in range(TOP_K):
            p = pos_ref[0, 0, t * TOP_K + k]
            pltpu.make_async_copy(ring.at[slot, pl.ds(t, 1)], xs_hbm.at[pl.ds(p, 1)],
                                  sem.at[slot]).start(priority=k % 2)
        return c

    lax.fori_loop(0, tt, issue, 0, unroll=2)

    def drain(sl):
        for _ in range(TOP_K):
            pltpu.make_async_copy(ring.at[sl], ring.at[sl], sem.at[sl]).wait()

    @pl.when(i == 0)
    def _():
        zeros[...] = jnp.zeros(zeros.shape, F32)

        def pad_copy(row):
            return pltpu.make_async_copy(zeros.at[pl.ds(0, 1)], xs_hbm.at[pl.ds(row, 1)], zsem)

        def tail_copy(b):
            dst = xs_hbm.at[pl.ds(pl.multiple_of(b * MOE_ROWS, MOE_ROWS), MOE_ROWS)]
            return pltpu.make_async_copy(zeros, dst, zsem)

        def per_expert(fn):
            def body(e, c):
                cnt = cnt_ref[e]
                first = start_ref[e] * MOE_ROWS + cnt
                lax.fori_loop(0, (-cnt) & (MOE_ROWS - 1), lambda j, cc: (fn(first + j), cc)[1], 0)
                return c
            lax.fori_loop(0, cnt_ref.shape[0], body, 0)

        def per_tail(fn):
            lax.fori_loop(n_used_ref[0], xs_hbm.shape[0] // MOE_ROWS, lambda b, cc: (fn(b), cc)[1], 0)

        per_expert(lambda row: pad_copy(row).start())
        per_tail(lambda b: tail_copy(b).start())
        per_expert(lambda row: pad_copy(row).wait())
        per_tail(lambda b: tail_copy(b).wait())

    @pl.when(i > 0)
    def _():
        drain(1 - slot)

    @pl.when(i == n - 1)
    def _():
        drain(slot)


def _dispatch(h, pos, pad_info, n_used, n_slots):
    T, D = h.shape
    blk_start, counts = pad_info
    tt = 256
    grid_spec = pltpu.PrefetchScalarGridSpec(
        num_scalar_prefetch=3,
        grid=(T // tt,),
        in_specs=[
            pl.BlockSpec((1, 1, tt * TOP_K), lambda i, s, c, u: (i, 0, 0), memory_space=pltpu.SMEM),
            pl.BlockSpec((tt, D), lambda i, s, c, u: (i, 0)),
        ],
        out_specs=pl.BlockSpec(memory_space=pl.ANY),
        scratch_shapes=[pltpu.VMEM((2, tt, D), F32), pltpu.VMEM((MOE_ROWS, D), F32),
                        pltpu.SemaphoreType.DMA((2,)), pltpu.SemaphoreType.DMA],
    )
    return pl.pallas_call(
        _dispatch_kernel,
        out_shape=jax.ShapeDtypeStruct((n_slots, D), F32),
        grid_spec=grid_spec,
        compiler_params=_cparams(("arbitrary",)),
        name="moe_dispatch",
    )(blk_start, counts, n_used, pos.reshape(T // tt, 1, tt * TOP_K), h)


def _moe_kernel(sb_e_ref, sb_blk0_ref, sb_nblk_ref, n_used_ref, xs_hbm, wg_ref, bg_ref, wu_ref, bu_ref,
                wd_ref, bd_ref, y_hbm, x_s, y_s, stage, sem_in, sem_out):
    s = pl.program_id(0)
    f = pl.program_id(1)
    n_s = pl.num_programs(0)
    n_f = pl.num_programs(1)
    nblk = sb_nblk_ref[s]
    blk0 = sb_blk0_ref[s]
    tf = wg_ref.shape[3]
    n_stage = stage.shape[0]

    def rows_of(blk, nb):
        return pl.ds(pl.multiple_of(blk * MOE_ROWS, MOE_ROWS), nb * MOE_ROWS)

    def in_copy(first_blk, r):
        return pltpu.make_async_copy(xs_hbm.at[rows_of(first_blk + r, 1)], stage.at[r & (n_stage - 1)],
                                     sem_in.at[r & (n_stage - 1)])

    def out_copy(first_blk, r):
        return pltpu.make_async_copy(y_s.at[rows_of(r, 1)], y_hbm.at[rows_of(first_blk + r, 1)], sem_out)

    def for_blocks(n, fn):
        def body(r, c):
            fn(r)
            return c
        lax.fori_loop(0, n, body, 0)

    def prefetch(first_blk, n):
        for_blocks(jnp.minimum(n, n_stage), lambda r: in_copy(first_blk, r).start())

    def ffn(blk, nb, first):
        rows = rows_of(blk, nb)
        if first:
            for i in range(nb):
                r = blk + i
                in_copy(blk0, r).wait()
                x_s[rows_of(r, 1), :] = stage[r & (n_stage - 1)].astype(BF16)

                @pl.when(r + n_stage < nblk)
                def _(r=r):
                    in_copy(blk0, r + n_stage).start()

        wgu = jnp.concatenate([wg_ref[0, 0].astype(BF16), wu_ref[0, 0].astype(BF16)], axis=1)
        gu = _dot(x_s[rows, :], wgu)
        g = jnp.minimum(gu[:, :tf] + bg_ref[0, 0], SWIGLU_LIMIT)
        u = jnp.clip(gu[:, tf:] + bu_ref[0, 0], -SWIGLU_LIMIT, SWIGLU_LIMIT)
        act = (u + 1.0) * g * jax.nn.sigmoid(SWIGLU_ALPHA * g)
        y = _dot(act.astype(BF16), wd_ref[0, 0].astype(BF16))
        if first:
            y_s[rows, :] = y + bd_ref[0, 0]
        else:
            y_s[rows, :] += y

    def all_chunks(first):
        n_big = lax.shift_right_logical(nblk, FFN_CHUNK_LOG2)
        for_blocks(n_big, lambda c: ffn(c * FFN_CHUNK, FFN_CHUNK, first))
        done = n_big * FFN_CHUNK
        nb = FFN_CHUNK // 2
        while nb >= 1:
            @pl.when((nblk & nb) != 0)
            def _(done=done, nb=nb):
                ffn(done, nb, first)
            done = done + (nblk & nb)
            nb //= 2

    @pl.when(nblk > 0)
    def _():
        @pl.when(f == 0)
        def _():
            @pl.when(s == 0)
            def _():
                prefetch(blk0, nblk)

            @pl.when(s > 0)
            def _():
                for_blocks(sb_nblk_ref[s - 1], lambda r: out_copy(sb_blk0_ref[s - 1], r).wait())

            all_chunks(True)

        @pl.when(f > 0)
        def _():
            all_chunks(False)

        @pl.when(f == n_f - 1)
        def _():
            for_blocks(nblk, lambda r: out_copy(blk0, r).start())
            nxt = jnp.minimum(s + 1, n_s - 1)
            has_next = jnp.logical_and(s + 1 < n_s, sb_nblk_ref[nxt] > 0)

            @pl.when(has_next)
            def _():
                prefetch(sb_blk0_ref[nxt], sb_nblk_ref[nxt])

            @pl.when(jnp.logical_not(has_next))
            def _():
                for_blocks(nblk, lambda r: out_copy(blk0, r).wait())

    @pl.when(jnp.logical_and(s == pl.num_programs(0) - 1, f == n_f - 1))
    def _():
        y_s[0:MOE_ROWS, :] = jnp.zeros((MOE_ROWS, y_s.shape[1]), F32)

        def tail_copy(r):
            dst = y_hbm.at[pl.ds(pl.multiple_of(r * MOE_ROWS, MOE_ROWS), MOE_ROWS)]
            return pltpu.make_async_copy(y_s.at[0:MOE_ROWS], dst, sem_out)

        def start(r, c):
            tail_copy(r).start()
            return c

        def wait(r, c):
            tail_copy(r).wait()
            return c

        lax.fori_loop(n_used_ref[0], y_hbm.shape[0] // MOE_ROWS, start, 0)
        lax.fori_loop(n_used_ref[0], y_hbm.shape[0] // MOE_ROWS, wait, 0)


def _moe_ffn(xs, plan, w_gate, b_gate, w_up, b_up, w_down, b_down, l, n_sb):
    sb_e, sb_blk0, sb_nblk, n_used = plan
    n_slots, D = xs.shape
    L, E, _, F = w_gate.shape
    tf = 256
    rows = SB_BLOCKS * MOE_ROWS
    n_f = F // tf

    def ftile(s, f, n):
        return jnp.where(n[s] > 0, f, n_f - 1)

    grid_spec = pltpu.PrefetchScalarGridSpec(
        num_scalar_prefetch=4,
        grid=(n_sb, n_f),
        in_specs=[
            pl.BlockSpec(memory_space=pl.ANY),
            pl.BlockSpec((1, 1, D, tf), lambda s, f, e, b, n, u: (l, e[s], 0, ftile(s, f, n))),
            pl.BlockSpec((1, 1, 1, tf), lambda s, f, e, b, n, u: (l, e[s], 0, ftile(s, f, n))),
            pl.BlockSpec((1, 1, D, tf), lambda s, f, e, b, n, u: (l, e[s], 0, ftile(s, f, n))),
            pl.BlockSpec((1, 1, 1, tf), lambda s, f, e, b, n, u: (l, e[s], 0, ftile(s, f, n))),
            pl.BlockSpec((1, 1, tf, D), lambda s, f, e, b, n, u: (l, e[s], ftile(s, f, n), 0)),
            pl.BlockSpec((1, 1, 1, D), lambda s, f, e, b, n, u: (l, e[s], 0, 0)),
        ],
        out_specs=pl.BlockSpec(memory_space=pl.ANY),
        scratch_shapes=[
            pltpu.VMEM((rows, D), BF16),
            pltpu.VMEM((rows, D), F32),
            pltpu.VMEM((XS_STAGES, MOE_ROWS, D), F32),
            pltpu.SemaphoreType.DMA((XS_STAGES,)),
            pltpu.SemaphoreType.DMA,
        ],
    )
    return pl.pallas_call(
        _moe_kernel,
        out_shape=jax.ShapeDtypeStruct((n_slots, D), F32),
        grid_spec=grid_spec,
        compiler_params=_cparams(("arbitrary", "arbitrary")),
        name="moe_ffn",
    )(sb_e, sb_blk0, sb_nblk, n_used, xs, w_gate, b_gate.reshape(L, E, 1, F), w_up, b_up.reshape(L, E, 1, F),
      w_down, b_down.reshape(L, E, 1, D))


def _combine_kernel(pos_ref, pos_next_ref, y_hbm, gates_ref, x_ref, mods_ref, g_ref, o_ref, stage, sem, *,
                    final_norm):
    i = pl.program_id(0)
    n_steps = pl.num_programs(0)
    n = x_ref.shape[0]
    slot = i & 1

    def fetch(p_ref, sl):
        def issue(t, c):
            for k in range(TOP_K):
                p = p_ref[0, 0, t * TOP_K + k]
                pltpu.make_async_copy(y_hbm.at[pl.ds(p, 1)], stage.at[sl, k, pl.ds(t, 1)],
                                      sem.at[sl]).start(priority=k % 2)
            return c
        lax.fori_loop(0, n, issue, 0, unroll=2)

    @pl.when(i == 0)
    def _():
        fetch(pos_ref, slot)

    @pl.when(i + 1 < n_steps)
    def _():
        fetch(pos_next_ref, 1 - slot)

    for k in range(TOP_K):
        pltpu.make_async_copy(stage.at[slot, k], stage.at[slot, k], sem.at[slot]).wait()
    gates = gates_ref[...]
    acc = gates[:, 0:1] * stage[slot, 0]
    for k in range(1, TOP_K):
        acc = acc + gates[:, k:k + 1] * stage[slot, k]
    res = x_ref[...] + mods_ref[0][5:6] * acc
    o_ref[...] = _rms_lanes(res, g_ref[...]) if final_norm else res


def _combine(y_slots, pos, gates, x, mods, g_final, final_norm, n_ctx_rows, dec_seq):
    T, D = x.shape
    cb = 128
    n_steps = T // cb
    return pl.pallas_call(
        functools.partial(_combine_kernel, final_norm=final_norm),
        out_shape=jax.ShapeDtypeStruct((T, D), F32),
        grid=(n_steps,),
        in_specs=[
            pl.BlockSpec((1, 1, cb * TOP_K), lambda i: (i, 0, 0), memory_space=pltpu.SMEM),
            pl.BlockSpec((1, 1, cb * TOP_K), lambda i: (jnp.minimum(i + 1, n_steps - 1), 0, 0),
                         memory_space=pltpu.SMEM),
            pl.BlockSpec(memory_space=pl.ANY),
            pl.BlockSpec((cb, LANES), lambda i: (i, 0)),
            pl.BlockSpec((cb, D), lambda i: (i, 0)),
            pl.BlockSpec((1, N_MOD, D), lambda i: (_cond_index(i, cb, n_ctx_rows, dec_seq), 0, 0)),
            pl.BlockSpec((1, D), lambda i: (0, 0)),
        ],
        out_specs=pl.BlockSpec((cb, D), lambda i: (i, 0)),
        scratch_shapes=[pltpu.VMEM((2, TOP_K, cb, D), F32), pltpu.SemaphoreType.DMA((2,))],
        compiler_params=_cparams(("arbitrary",)),
        name="moe_combine",
    )(pos.reshape(n_steps, 1, cb * TOP_K), pos.reshape(n_steps, 1, cb * TOP_K), y_slots, gates, x, mods, g_final)


def _rope_tables(n, head_dim):
    quarter = head_dim // 4
    inv = ROPE_THETA ** (-np.arange(quarter, dtype=np.float64) / quarter)
    pos = np.arange(n)
    row = (pos // GRID_W)[:, None] * inv
    col = (pos % GRID_W)[:, None] * inv
    cos = np.concatenate([np.cos(row), np.cos(row), np.cos(col), np.cos(col)], axis=1)
    sin = np.concatenate([-np.sin(row), np.sin(row), -np.sin(col), np.sin(col)], axis=1)
    reps = LANES // head_dim
    return (jnp.asarray(np.tile(cos, (1, reps)), dtype=F32), jnp.asarray(np.tile(sin, (1, reps)), dtype=F32))


def kernel(x_prompt, x_sample, c, cache_k_a, cache_v_a, cache_k_c, cache_v_c, c_ctx, w_mod, b_mod, g_norm1, g_norm2, w_in, lam_q1, lam_k1, lam_q2, lam_k2, g_subln_a, g_q_c, g_k_c, g_v_b, w_s_b, b_s_b, w_d, s_d, w_out, w_router, b_router, w_gate, b_gate, w_up, b_up, w_down, b_down, g_final):
    B, S, D = x_prompt.shape
    DB, DS, _ = x_sample.shape
    L = w_mod.shape[0]
    P = cache_k_a.shape[2]
    E = w_router.shape[2]
    n_ctx = B * S
    T = n_ctx + DB * DS
    GW = w_out.shape[1] // 4
    dk_a = cache_k_a.shape[-1]
    hd_c = cache_k_c.shape[-1]

    x = jnp.concatenate([x_prompt.reshape(n_ctx, D), x_sample.reshape(DB * DS, D)], axis=0)
    cond8 = jnp.zeros((8, D), F32).at[0].set(c_ctx).at[1:1 + DB].set(c)
    mod = _modulation(cond8, w_mod, b_mod).reshape(L, 8, N_MOD, D)

    cos_a, sin_a = _rope_tables(DS, dk_a)
    cos_c, sin_c = _rope_tables(DS, hd_c)
    caches_all = (cache_k_a.reshape(DB, L, P, GW), cache_v_a.reshape(DB, L, P, GW),
                  cache_k_c.reshape(DB, L, P, GW // 2), cache_v_c.reshape(DB, L, P, GW // 2))

    w_in = _to_bf16(w_in)
    w_out = _to_bf16(w_out)
    n_slots_max = -(-(T * TOP_K + E * (MOE_ROWS - 1)) // MOE_ROWS)
    n_sb = E + n_slots_max // SB_BLOCKS

    states = []
    for l in range(L):
        mods = mod[l]
        lam0 = 0.8 - 0.6 * math.exp(-0.3 * l)
        proj = _inproj(x, g_norm1[l][None, :], mods, w_in, l, n_ctx, DS)
        params = (jnp.stack([lam_q1[l], lam_k1[l], lam_q2[l], lam_k2[l]]), g_subln_a[l][None, :],
                  g_q_c[l][None, :], g_k_c[l][None, :], g_v_b[l][None, :], w_s_b[l], b_s_b[l].T,
                  w_d[l], s_d[l][None, :])
        mix_c, kc_norm = _mixer(proj, l, 0, B, S, S, lam0, params)
        mix_s = _mixer(proj, l, n_ctx, DB, DS, CHUNK, lam0, params, caches=caches_all,
                       rope_tabs=(cos_a, sin_a, cos_c, sin_c))
        pc = proj[:n_ctx]
        states.append((pc[:, GW:2 * GW], pc[:, 2 * GW:3 * GW], kc_norm, pc[:, 6 * GW + GW // 2:7 * GW]))
        x = _outproj(mix_c, mix_s, w_out, x, mods, l, n_ctx, DS)
        h, top_idx, gates = _router(x, g_norm2[l][None, :], mods, w_router[l], b_router[l], n_ctx, DS)
        pos, plan, pad_info = _route_plan(top_idx[:, :TOP_K], E, n_sb)
        xs = _dispatch(h, pos, pad_info, plan[3], n_slots_max * MOE_ROWS)
        y_slots = _moe_ffn(xs, plan, w_gate, b_gate, w_up, b_up, w_down, b_down, l, n_sb)
        x = _combine(y_slots, pos, gates, x, mods, g_final[None, :], l == L - 1, n_ctx, DS)

    y = x
    y_prompt = y[:n_ctx].reshape(B, S, D)
    y_sample = y[n_ctx:].reshape(DB, DS, D)
    h_a = GW // (2 * dk_a)
    new_k_a = jnp.stack([st[0].reshape(B, S, h_a, 2, dk_a) for st in states], axis=1)
    new_v_a = jnp.stack([st[1].reshape(B, S, h_a, 2 * dk_a) for st in states], axis=1)
    new_k_c = jnp.stack([st[2].reshape(B, S, GW // 2 // hd_c, hd_c) for st in states], axis=1)
    new_v_c = jnp.stack([st[3].reshape(B, S, GW // 2 // hd_c, hd_c) for st in states], axis=1)
    return (y_prompt, y_sample, new_k_a, new_v_a, new_k_c, new_v_c)
```

```python
import functools
import math

import jax
import jax.numpy as jnp
import numpy as np
from jax import lax
from jax.experimental import pallas as pl
from jax.experimental.pallas import tpu as pltpu

F32 = jnp.float32
BF16 = jnp.bfloat16

EPS = 1e-6
GRID_W = 64
ROPE_THETA = 10000.0
N_MOD = 6
TOP_K = 4
SWIGLU_LIMIT = 7.0
SWIGLU_ALPHA = 1.702
POOL_WINDOWS = (2, 4, 8, 16)

LANES = 128
CHUNK = 128
MOE_ROWS = 128
SB_BLOCKS = 16
XS_STAGES = 8
FFN_CHUNK_LOG2 = 3
FFN_CHUNK = 1 << FFN_CHUNK_LOG2
VMEM_LIMIT = 56 * 1024 * 1024


def _cparams(sem, vmem=VMEM_LIMIT):
    return pltpu.CompilerParams(dimension_semantics=sem, vmem_limit_bytes=vmem)


def _dot(a, b):
    return jnp.dot(a, b, preferred_element_type=F32)


def _dot_nt(a, b):
    return lax.dot_general(a, b, (((1,), (1,)), ((), ())), preferred_element_type=F32)


def _rms_lanes(x, g):
    ms = jnp.mean(x * x, axis=-1, keepdims=True)
    return x * lax.rsqrt(ms + EPS) * g


def _mod_kernel(cond_ref, w_ref, b_ref, o_ref):
    c = cond_ref[...]
    a = (c * jax.nn.sigmoid(c)).astype(BF16)
    o_ref[0] = _dot(a, w_ref[0].astype(BF16)) + b_ref[0]


def _modulation(cond8, w_mod, b_mod):
    L, D, NM = w_mod.shape
    tn = 1024
    return pl.pallas_call(
        _mod_kernel,
        out_shape=jax.ShapeDtypeStruct((L, 8, NM), F32),
        grid=(L, NM // tn),
        in_specs=[
            pl.BlockSpec((8, D), lambda l, n: (0, 0)),
            pl.BlockSpec((1, D, tn), lambda l, n: (l, 0, n)),
            pl.BlockSpec((1, 1, tn), lambda l, n: (l, 0, n)),
        ],
        out_specs=pl.BlockSpec((1, 8, tn), lambda l, n: (l, 0, n)),
        compiler_params=_cparams(("arbitrary", "arbitrary")),
        name="modulation",
    )(cond8, w_mod, b_mod.reshape(L, 1, NM))


def _cast_kernel(w_ref, o_ref):
    o_ref[...] = w_ref[...].astype(BF16)


def _to_bf16(w):
    L, K, N = w.shape
    tn = 512
    return pl.pallas_call(
        _cast_kernel,
        out_shape=jax.ShapeDtypeStruct(w.shape, BF16),
        grid=(L, N // tn),
        in_specs=[pl.BlockSpec((1, K, tn), lambda l, n: (l, 0, n))],
        out_specs=pl.BlockSpec((1, K, tn), lambda l, n: (l, 0, n)),
        compiler_params=_cparams(("arbitrary", "arbitrary")),
        name="weights_to_bf16",
    )(w)


def _cond_index(row_block, rows_per_block, n_ctx_rows, dec_seq):
    r0 = row_block * rows_per_block
    return jnp.where(r0 < n_ctx_rows, 0, 1 + (r0 - n_ctx_rows) // dec_seq)


def _inproj_kernel(x_ref, g_ref, mods_ref, w_ref, o_ref, h_ref, *, shift_row, scale_row, chunk):
    @pl.when(pl.program_id(1) == 0)
    def _():
        m = mods_ref[0]
        shift = m[shift_row:shift_row + 1]
        scale1 = 1.0 + m[scale_row:scale_row + 1]
        g = g_ref[...]

        def body(i, carry):
            r = pl.multiple_of(i * chunk, chunk)
            x = x_ref[pl.ds(r, chunk), :]
            h_ref[pl.ds(r, chunk), :] = (_rms_lanes(x, g) * scale1 + shift).astype(BF16)
            return carry

        lax.fori_loop(0, x_ref.shape[0] // chunk, body, 0)

    o_ref[...] = _dot(h_ref[...], w_ref[0])


def _inproj(x, token_row0, g, mods, w_in, l, n_ctx_rows, dec_seq):
    n_rows, D = x.shape
    PW = w_in.shape[2]
    tm, tn = 1024, 1024
    tb0 = token_row0 // tm
    kern = functools.partial(_inproj_kernel, shift_row=0, scale_row=1, chunk=128)
    return pl.pallas_call(
        kern,
        out_shape=jax.ShapeDtypeStruct((n_rows, PW), F32),
        grid=(n_rows // tm, PW // tn),
        in_specs=[
            pl.BlockSpec((tm, D), lambda m, n: (m, 0)),
            pl.BlockSpec((1, D), lambda m, n: (0, 0)),
            pl.BlockSpec((1, N_MOD, D), lambda m, n: (_cond_index(tb0 + m, tm, n_ctx_rows, dec_seq), 0, 0)),
            pl.BlockSpec((1, D, tn), lambda m, n: (l, 0, n)),
        ],
        out_specs=pl.BlockSpec((tm, tn), lambda m, n: (m, n)),
        scratch_shapes=[pltpu.VMEM((tm, D), BF16)],
        compiler_params=_cparams(("arbitrary", "arbitrary")),
        name="inproj",
    )(x, g, mods, w_in)


def _rope(x, cos, sin, q):
    lane = lax.broadcasted_iota(jnp.int32, x.shape, 1)
    first_half = (lane & q) == 0
    partner = jnp.where(first_half, pltpu.roll(x, LANES - q, 1), pltpu.roll(x, q, 1))
    return x * cos + partner * sin


def _exp_scores(s):
    return jnp.exp(s - jnp.max(s, axis=-1, keepdims=True)).astype(BF16)


def _mixer_kernel(*refs, seq, qb, n_cache, lam0, rope):
    it = iter(refs)
    qa_ref, ub_ref, vb_ref, qc_ref = next(it), next(it), next(it), next(it)
    ka_ref, va_ref, kv_ref, pd_ref = next(it), next(it), next(it), next(it)
    if n_cache:
        cka_ref, cva_ref, ckc_ref, cvc_ref = next(it), next(it), next(it), next(it)
    if rope:
        cos_a_ref, sin_a_ref, cos_c_ref, sin_c_ref = next(it), next(it), next(it), next(it)
    lamp_ref, gsub_ref, gq_ref, gk_ref, gvb_ref = next(it), next(it), next(it), next(it), next(it)
    ws_ref, bst_ref, wd_ref, sd_ref = next(it), next(it), next(it), next(it)
    mix_ref = next(it)
    kas_ref, vas_ref, kcn_ref, vcs_ref = (None,) * 4 if n_cache else (next(it), next(it), next(it), next(it))
    ka_s, va_s, kc_s, vc_s = next(it), next(it), next(it), next(it)

    j = pl.program_id(1)
    row0 = pl.multiple_of(j * qb, qb)
    n_heads = qa_ref.shape[1] // LANES
    n_kv = kc_s.shape[1] // LANES
    dk_a = LANES // 2

    @pl.when(j == 0)
    def _():
        prep = 256

        def put_values(dst, rows, b, v):
            dst[rows, 2 * b * LANES:(2 * b + 1) * LANES] = v.astype(BF16)
            dst[rows, (2 * b + 1) * LANES:(2 * b + 2) * LANES] = jnp.ones(v.shape, BF16)

        for c in range(seq // prep):
            rows = slice(c * prep, (c + 1) * prep)
            for b in range(n_heads):
                cols = slice(b * LANES, (b + 1) * LANES)
                k = ka_ref[rows, cols]
                if rope:
                    k = _rope(k, cos_a_ref[rows, :], sin_a_ref[rows, :], dk_a // 4)
                ka_s[rows, cols] = k.astype(BF16)
                v = va_ref[rows, cols]
                put_values(va_s, rows, b, v)
                if kas_ref is not None:
                    kas_ref[rows, cols] = k
                    vas_ref[rows, cols] = v
            for b in range(n_kv):
                cols = slice(b * LANES, (b + 1) * LANES)
                k = _rms_lanes(kv_ref[rows, cols], gk_ref[...])
                v = kv_ref[rows, (n_kv + b) * LANES:(n_kv + b + 1) * LANES]
                if kcn_ref is not None:
                    kcn_ref[rows, cols] = k
                    vcs_ref[rows, cols] = v
                if rope:
                    k = _rope(k, cos_c_ref[rows, :], sin_c_ref[rows, :], LANES // 4)
                kc_s[rows, cols] = k.astype(BF16)
                put_values(vc_s, rows, b, v)
        if n_cache:
            tail = slice(seq, seq + n_cache)
            ka_s[tail, :] = cka_ref[0].astype(BF16)
            kc_s[tail, :] = ckc_ref[0].astype(BF16)
            for b in range(n_heads):
                put_values(va_s, tail, b, cva_ref[0, :, b * LANES:(b + 1) * LANES])
            for b in range(n_kv):
                put_values(vc_s, tail, b, cvc_ref[0, :, b * LANES:(b + 1) * LANES])

    lane = lax.broadcasted_iota(jnp.int32, (qb, LANES), 1)

    lp = lamp_ref[...]
    lam = (jnp.exp(jnp.sum(lp[0:1] * lp[1:2], axis=-1, keepdims=True))
           - jnp.exp(jnp.sum(lp[2:3] * lp[3:4], axis=-1, keepdims=True)) + lam0)
    scale_a = dk_a ** -0.5
    if rope:
        cos_a = cos_a_ref[pl.ds(row0, qb), :]
        sin_a = sin_a_ref[pl.ds(row0, qb), :]
        cos_c = cos_c_ref[pl.ds(row0, qb), :]
        sin_c = sin_c_ref[pl.ds(row0, qb), :]
    for h in range(n_heads):
        cols = slice(h * LANES, (h + 1) * LANES)
        q = qa_ref[:, cols]
        if rope:
            q = _rope(q, cos_a, sin_a, dk_a // 4)
        q = q * scale_a
        k = ka_s[:, cols]
        v1 = va_s[:, 2 * h * LANES:(2 * h + 2) * LANES]
        q0 = jnp.where(lane < dk_a, q, 0.0).astype(BF16)
        q1 = jnp.where(lane >= dk_a, q, 0.0).astype(BF16)
        o0 = _dot(_exp_scores(_dot_nt(q0, k)), v1)
        o1 = _dot(_exp_scores(_dot_nt(q1, k)), v1)
        o = o0[:, :LANES] * (1.0 / o0[:, LANES:LANES + 1]) - o1[:, :LANES] * (lam / o1[:, LANES:LANES + 1])
        mix_ref[:, cols] = (_rms_lanes(o, gsub_ref[...]) * (1.0 - lam0)).astype(BF16)

    off_b = n_heads * LANES
    for g in range(ub_ref.shape[1] // LANES):
        cols = slice(g * LANES, (g + 1) * LANES)
        vn = _rms_lanes(vb_ref[:, cols], gvb_ref[:, cols]).astype(BF16)
        w_s = ws_ref[g].astype(BF16)
        for c in range(qb // CHUNK):
            rows = slice(c * CHUNK, (c + 1) * CHUNK)
            mixed = _dot(w_s, vn[rows, :]) + bst_ref[:, g:g + 1]
            mix_ref[rows, off_b + g * LANES: off_b + (g + 1) * LANES] = (ub_ref[rows, cols] * mixed).astype(BF16)

    off_c = off_b + ub_ref.shape[1]
    scale_c = LANES ** -0.5
    for h in range(n_heads):
        cols = slice(h * LANES, (h + 1) * LANES)
        q = _rms_lanes(qc_ref[:, cols], gq_ref[...])
        if rope:
            q = _rope(q, cos_c, sin_c, LANES // 4)
        kv = h // (n_heads // n_kv)
        e = _exp_scores(_dot_nt((q * scale_c).astype(BF16), kc_s[:, kv * LANES:(kv + 1) * LANES]))
        o = _dot(e, vc_s[:, 2 * kv * LANES:(2 * kv + 2) * LANES])
        mix_ref[:, off_c + h * LANES: off_c + (h + 1) * LANES] = (
            o[:, :LANES] * (1.0 / o[:, LANES:LANES + 1])).astype(BF16)

    off_d = off_c + qc_ref.shape[1]
    win = min(qb + 2 * CHUNK, seq)
    start = pl.multiple_of(jnp.clip(row0 - CHUNK, 0, seq - win), CHUNK)
    t = row0 + lax.broadcasted_iota(jnp.int32, (qb, win), 0)
    col = start + lax.broadcasted_iota(jnp.int32, (qb, win), 1)
    t1 = row0 + lax.broadcasted_iota(jnp.int32, (qb, 1), 0)
    for g, w in enumerate(POOL_WINDOWS):
        cols = slice(g * LANES, (g + 1) * LANES)
        left, right = w // 2, w - 1 - w // 2
        band = jnp.where((col >= t - left) & (col <= t + right), 1.0, 0.0).astype(BF16)
        cnt = (jnp.minimum(t1 + right + 1, seq) - jnp.maximum(t1 - left, 0)).astype(F32)
        xw = pd_ref[pl.ds(start, win), cols]
        hi = xw.astype(BF16)
        lo = (xw - hi.astype(F32)).astype(BF16)
        wsum = _dot(band, hi) + _dot(band, lo)
        xg = pd_ref[pl.ds(row0, qb), cols]
        pooled = wsum / cnt - xg
        y = _dot(pooled.astype(BF16), wd_ref[g].astype(BF16)) * sd_ref[:, cols]
        mix_ref[:, off_d + g * LANES: off_d + (g + 1) * LANES] = y.astype(BF16)


def _mixer(proj, l, n_seq, seq, qb, lam0, params, caches=None, rope_tabs=None):
    T, PW = proj.shape
    GW = PW // 8
    n_qb = seq // qb
    n_cache = 0 if caches is None else caches[0].shape[2]
    n_keys = seq + n_cache

    def qspec(col):
        return pl.BlockSpec((qb, GW), lambda s, j: (s * n_qb + j, col))

    def sspec(col):
        return pl.BlockSpec((seq, GW), lambda s, j: (s, col))

    def full(shape):
        nd = len(shape)
        return pl.BlockSpec(shape, lambda s, j: (0,) * nd)

    in_specs = [qspec(0), qspec(3), qspec(4), qspec(5), sspec(1), sspec(2), sspec(6), sspec(7)]
    args = [proj] * 8
    if caches is not None:
        for c in caches:
            in_specs.append(pl.BlockSpec((None, 1, c.shape[2], c.shape[3]), lambda s, j: (s, l, 0, 0)))
            args.append(c)
    if rope_tabs is not None:
        for tab in rope_tabs:
            in_specs.append(full(tab.shape))
            args.append(tab)
    for p in params:
        in_specs.append(full(p.shape))
        args.append(p)

    MW = 4 * GW
    mix_spec = pl.BlockSpec((qb, MW), lambda s, j: (s * n_qb + j, 0))
    if caches is None:
        state_w = (GW, GW, GW // 2, GW // 2)
        out_shape = (jax.ShapeDtypeStruct((n_seq * seq, MW), BF16),
                     *(jax.ShapeDtypeStruct((n_seq * seq, w), F32) for w in state_w))
        out_specs = (mix_spec, *(pl.BlockSpec((seq, w), lambda s, j: (s, 0)) for w in state_w))
    else:
        out_shape = jax.ShapeDtypeStruct((n_seq * seq, MW), BF16)
        out_specs = mix_spec

    kern = functools.partial(_mixer_kernel, seq=seq, qb=qb, n_cache=n_cache, lam0=lam0,
                             rope=rope_tabs is not None)
    return pl.pallas_call(
        kern,
        out_shape=out_shape,
        grid=(n_seq, n_qb),
        in_specs=in_specs,
        out_specs=out_specs,
        scratch_shapes=[pltpu.VMEM((n_keys, GW), BF16), pltpu.VMEM((n_keys, 2 * GW), BF16),
                        pltpu.VMEM((n_keys, GW // 2), BF16), pltpu.VMEM((n_keys, GW), BF16)],
        compiler_params=_cparams(("arbitrary", "arbitrary")),
        name="mixer_latent" if caches is not None else "mixer_context",
    )(*args)


def _outproj_kernel(mix_c_ref, mix_s_ref, w_ref, x_c_ref, x_s_ref, mods_ref, o_ref, *, gate_row, n_ctx_tiles):
    tn = o_ref.shape[1]
    gcols = pl.ds(pl.multiple_of(pl.program_id(1) * tn, tn), tn)
    gate = mods_ref[0, gate_row:gate_row + 1, gcols]
    w = w_ref[0]
    is_ctx = pl.program_id(0) < n_ctx_tiles

    @pl.when(is_ctx)
    def _():
        o_ref[...] = x_c_ref[...] + gate * _dot(mix_c_ref[...], w)

    @pl.when(jnp.logical_not(is_ctx))
    def _():
        o_ref[...] = x_s_ref[...] + gate * _dot(mix_s_ref[...], w)


def _outproj(mix_c, mix_s, w_out, x_c, x_s, mods, l, n_ctx_rows, dec_seq):
    T = mix_c.shape[0] + mix_s.shape[0]
    D = x_c.shape[1]
    MW = mix_c.shape[1]
    tm, tn = 1024, 512
    nct = n_ctx_rows // tm
    kern = functools.partial(_outproj_kernel, gate_row=2, n_ctx_tiles=nct)
    return pl.pallas_call(
        kern,
        out_shape=jax.ShapeDtypeStruct((T, D), F32),
        grid=(T // tm, D // tn),
        in_specs=[
            pl.BlockSpec((tm, MW), lambda m, n: (jnp.minimum(m, nct - 1), 0)),
            pl.BlockSpec((tm, MW), lambda m, n: (jnp.maximum(m - nct, 0), 0)),
            pl.BlockSpec((1, MW, tn), lambda m, n: (l, 0, n)),
            pl.BlockSpec((tm, tn), lambda m, n: (jnp.minimum(m, nct - 1), jnp.where(m < nct, n, D // tn - 1))),
            pl.BlockSpec((tm, tn), lambda m, n: (jnp.maximum(m - nct, 0), jnp.where(m < nct, 0, n))),
            pl.BlockSpec((1, N_MOD, D), lambda m, n: (_cond_index(m, tm, n_ctx_rows, dec_seq), 0, 0)),
        ],
        out_specs=pl.BlockSpec((tm, tn), lambda m, n: (m, n)),
        compiler_params=_cparams(("arbitrary", "arbitrary")),
        name="outproj",
    )(mix_c, mix_s, w_out, x_c, x_s, mods)


def _split_bf16(x):
    hi = x.astype(BF16)
    return hi, (x - hi.astype(F32)).astype(BF16)


def _router_kernel(x_ref, g_ref, mods_ref, wr_ref, br_ref, h_ref, idx_ref, gate_ref, *, n_experts):
    m = mods_ref[0]
    h = _rms_lanes(x_ref[...], g_ref[...]) * (1.0 + m[4:5]) + m[3:4]
    h_ref[...] = h
    h_hi, h_lo = _split_bf16(h)
    w_hi, w_lo = _split_bf16(wr_ref[...])
    logits = _dot(h_hi, w_hi) + _dot(h_hi, w_lo) + _dot(h_lo, w_hi) + br_ref[...]
    lane = lax.broadcasted_iota(jnp.int32, logits.shape, 1)
    lane_f = lane.astype(F32)
    neg = jnp.float32(-jnp.inf)
    logits = jnp.where(lane < n_experts, logits, neg)
    idx_out = jnp.zeros(logits.shape, F32)
    val_out = jnp.zeros(logits.shape, F32)
    top = None
    denom = None
    for k in range(TOP_K):
        v = jnp.max(logits, axis=-1, keepdims=True)
        i = jnp.min(jnp.where(logits == v, lane_f, float(LANES)), axis=-1, keepdims=True)
        if k == 0:
            top = v
        e = jnp.exp(v - top)
        denom = e if denom is None else denom + e
        idx_out = jnp.where(lane == k, i, idx_out)
        val_out = jnp.where(lane == k, e, val_out)
        logits = jnp.where(lane_f == i, neg, logits)
    idx_ref[...] = idx_out.astype(jnp.int32)
    gate_ref[...] = val_out / denom


def _router(x, g, mods, w_router_l, b_router_l, n_ctx_rows, dec_seq):
    T, D = x.shape
    E = w_router_l.shape[1]
    tm = 512
    wr = jnp.zeros((D, LANES), F32).at[:, :E].set(w_router_l)
    br = jnp.zeros((1, LANES), F32).at[0, :E].set(b_router_l)
    kern = functools.partial(_router_kernel, n_experts=E)
    return pl.pallas_call(
        kern,
        out_shape=(jax.ShapeDtypeStruct((T, D), F32),
                   jax.ShapeDtypeStruct((T, LANES), jnp.int32),
                   jax.ShapeDtypeStruct((T, LANES), F32)),
        grid=(T // tm,),
        in_specs=[
            pl.BlockSpec((tm, D), lambda m: (m, 0)),
            pl.BlockSpec((1, D), lambda m: (0, 0)),
            pl.BlockSpec((1, N_MOD, D), lambda m: (_cond_index(m, tm, n_ctx_rows, dec_seq), 0, 0)),
            pl.BlockSpec((D, LANES), lambda m: (0, 0)),
            pl.BlockSpec((1, LANES), lambda m: (0, 0)),
        ],
        out_specs=(pl.BlockSpec((tm, D), lambda m: (m, 0)),
                   pl.BlockSpec((tm, LANES), lambda m: (m, 0)),
                   pl.BlockSpec((tm, LANES), lambda m: (m, 0))),
        compiler_params=_cparams(("arbitrary",)),
        name="norm_router",
    )(x, g, mods, wr, br)


def _route_plan(top_idx, n_experts, n_sb):
    T, K = top_idx.shape
    flat_e = top_idx.reshape(-1)
    onehot = (flat_e[:, None] == jnp.arange(n_experts, dtype=jnp.int32)[None, :]).astype(jnp.int32)
    csum = jnp.cumsum(onehot, axis=0)
    counts = csum[-1]
    nblk = (counts + MOE_ROWS - 1) // MOE_ROWS
    blk_start = jnp.cumsum(nblk) - nblk
    pos = jnp.sum(onehot * (csum - 1 + (blk_start * MOE_ROWS)[None, :]), axis=1)
    nsb = (nblk + SB_BLOCKS - 1) // SB_BLOCKS
    sb_end = jnp.cumsum(nsb)
    s = jnp.arange(n_sb, dtype=jnp.int32)
    total = sb_end[-1]
    sc = jnp.minimum(s, total - 1)
    e = jnp.sum((sb_end[None, :] <= sc[:, None]).astype(jnp.int32), axis=1)
    k = sc - (sb_end[e] - nsb[e])
    sb_blk0 = (blk_start[e] + k * SB_BLOCKS).astype(jnp.int32)
    sb_nblk = jnp.where(s < total, jnp.minimum(SB_BLOCKS, nblk[e] - k * SB_BLOCKS), 0).astype(jnp.int32)
    n_used = jnp.sum(nblk).astype(jnp.int32).reshape(1)
    plan = (e, sb_blk0, sb_nblk, n_used)
    return pos.reshape(T, K).astype(jnp.int32), plan, (blk_start.astype(jnp.int32), counts.astype(jnp.int32))


def _dispatch_kernel(start_ref, cnt_ref, n_used_ref, pos_ref, h_ref, xs_hbm, ring, zeros, sem, zsem):
    i = pl.program_id(0)
    n = pl.num_programs(0)
    tt = h_ref.shape[0]
    slot = i & 1

    ring[slot] = h_ref[...]

    def issue(t, c):
        for k in range(TOP_K):
            p = pos_ref[0, 0, t * TOP_K + k]
            pltpu.make_async_copy(ring.at[slot, pl.ds(t, 1)], xs_hbm.at[pl.ds(p, 1)],
                                  sem.at[slot]).start(priority=k % 2)
        return c

    lax.fori_loop(0, tt, issue, 0, unroll=2)

    def drain(sl):
        for _ in range(TOP_K):
            pltpu.make_async_copy(ring.at[sl], ring.at[sl], sem.at[sl]).wait()

    @pl.when(i == 0)
    def _():
        zeros[...] = jnp.zeros(zeros.shape, F32)

        def pad_copy(row):
            return pltpu.make_async_copy(zeros.at[pl.ds(0, 1)], xs_hbm.at[pl.ds(row, 1)], zsem)

        def tail_copy(b):
            dst = xs_hbm.at[pl.ds(pl.multiple_of(b * MOE_ROWS, MOE_ROWS), MOE_ROWS)]
            return pltpu.make_async_copy(zeros, dst, zsem)

        def per_expert(fn):
            def body(e, c):
                cnt = cnt_ref[e]
                first = start_ref[e] * MOE_ROWS + cnt
                lax.fori_loop(0, (-cnt) & (MOE_ROWS - 1), lambda j, cc: (fn(first + j), cc)[1], 0)
                return c
            lax.fori_loop(0, cnt_ref.shape[0], body, 0)

        def per_tail(fn):
            lax.fori_loop(n_used_ref[0], xs_hbm.shape[0] // MOE_ROWS, lambda b, cc: (fn(b), cc)[1], 0)

        per_expert(lambda row: pad_copy(row).start())
        per_tail(lambda b: tail_copy(b).start())
        per_expert(lambda row: pad_copy(row).wait())
        per_tail(lambda b: tail_copy(b).wait())

    @pl.when(i > 0)
    def _():
        drain(1 - slot)

    @pl.when(i == n - 1)
    def _():
        drain(slot)


def _dispatch(h, pos, pad_info, n_used, n_slots):
    T, D = h.shape
    blk_start, counts = pad_info
    tt = 256
    grid_spec = pltpu.PrefetchScalarGridSpec(
        num_scalar_prefetch=3,
        grid=(T // tt,),
        in_specs=[
            pl.BlockSpec((1, 1, tt * TOP_K), lambda i, s, c, u: (i, 0, 0), memory_space=pltpu.SMEM),
            pl.BlockSpec((tt, D), lambda i, s, c, u: (i, 0)),
        ],
        out_specs=pl.BlockSpec(memory_space=pl.ANY),
        scratch_shapes=[pltpu.VMEM((2, tt, D), F32), pltpu.VMEM((MOE_ROWS, D), F32),
                        pltpu.SemaphoreType.DMA((2,)), pltpu.SemaphoreType.DMA],
    )
    return pl.pallas_call(
        _dispatch_kernel,
        out_shape=jax.ShapeDtypeStruct((n_slots, D), F32),
        grid_spec=grid_spec,
        compiler_params=_cparams(("arbitrary",)),
        name="moe_dispatch",
    )(blk_start, counts, n_used, pos.reshape(T // tt, 1, tt * TOP_K), h)


def _moe_kernel(sb_e_ref, sb_blk0_ref, sb_nblk_ref, n_used_ref, xs_hbm, wg_ref, bg_ref, wu_ref, bu_ref,
                wd_ref, bd_ref, y_hbm, x_s, y_s, stage, sem_in, sem_out):
    s = pl.program_id(0)
    f = pl.program_id(1)
    n_s = pl.num_programs(0)
    n_f = pl.num_programs(1)
    nblk = sb_nblk_ref[s]
    blk0 = sb_blk0_ref[s]
    tf = wg_ref.shape[3]
    n_stage = stage.shape[0]

    def rows_of(blk, nb):
        return pl.ds(pl.multiple_of(blk * MOE_ROWS, MOE_ROWS), nb * MOE_ROWS)

    def in_copy(first_blk, r):
        return pltpu.make_async_copy(xs_hbm.at[rows_of(first_blk + r, 1)], stage.at[r & (n_stage - 1)],
                                     sem_in.at[r & (n_stage - 1)])

    def out_copy(first_blk, r):
        return pltpu.make_async_copy(y_s.at[rows_of(r, 1)], y_hbm.at[rows_of(first_blk + r, 1)], sem_out)

    def for_blocks(n, fn):
        def body(r, c):
            fn(r)
            return c
        lax.fori_loop(0, n, body, 0)

    def prefetch(first_blk, n):
        for_blocks(jnp.minimum(n, n_stage), lambda r: in_copy(first_blk, r).start())

    def ffn(blk, nb, first):
        rows = rows_of(blk, nb)
        if first:
            for i in range(nb):
                r = blk + i
                in_copy(blk0, r).wait()
                x_s[rows_of(r, 1), :] = stage[r & (n_stage - 1)].astype(BF16)

                @pl.when(r + n_stage < nblk)
                def _(r=r):
                    in_copy(blk0, r + n_stage).start()

        wgu = jnp.concatenate([wg_ref[0, 0].astype(BF16), wu_ref[0, 0].astype(BF16)], axis=1)
        gu = _dot(x_s[rows, :], wgu)
        g = jnp.minimum(gu[:, :tf] + bg_ref[0, 0], SWIGLU_LIMIT)
        u = jnp.clip(gu[:, tf:] + bu_ref[0, 0], -SWIGLU_LIMIT, SWIGLU_LIMIT)
        act = (u + 1.0) * g * jax.nn.sigmoid(SWIGLU_ALPHA * g)
        y = _dot(act.astype(BF16), wd_ref[0, 0].astype(BF16))
        if first:
            y_s[rows, :] = y + bd_ref[0, 0]
        else:
            y_s[rows, :] += y

    def all_chunks(first):
        n_big = lax.shift_right_logical(nblk, FFN_CHUNK_LOG2)
        for_blocks(n_big, lambda c: ffn(c * FFN_CHUNK, FFN_CHUNK, first))
        done = n_big * FFN_CHUNK
        nb = FFN_CHUNK // 2
        while nb >= 1:
            @pl.when((nblk & nb) != 0)
            def _(done=done, nb=nb):
                ffn(done, nb, first)
            done = done + (nblk & nb)
            nb //= 2

    @pl.when(nblk > 0)
    def _():
        @pl.when(f == 0)
        def _():
            @pl.when(s == 0)
            def _():
                prefetch(blk0, nblk)

            @pl.when(s > 0)
            def _():
                for_blocks(sb_nblk_ref[s - 1], lambda r: out_copy(sb_blk0_ref[s - 1], r).wait())

            all_chunks(True)

        @pl.when(f > 0)
        def _():
            all_chunks(False)

        @pl.when(f == n_f - 1)
        def _():
            for_blocks(nblk, lambda r: out_copy(blk0, r).start())
            nxt = jnp.minimum(s + 1, n_s - 1)
            has_next = jnp.logical_and(s + 1 < n_s, sb_nblk_ref[nxt] > 0)

            @pl.when(has_next)
            def _():
                prefetch(sb_blk0_ref[nxt], sb_nblk_ref[nxt])

            @pl.when(jnp.logical_not(has_next))
            def _():
                for_blocks(nblk, lambda r: out_copy(blk0, r).wait())

    @pl.when(jnp.logical_and(s == pl.num_programs(0) - 1, f == n_f - 1))
    def _():
        y_s[0:MOE_ROWS, :] = jnp.zeros((MOE_ROWS, y_s.shape[1]), F32)

        def tail_copy(r):
            dst = y_hbm.at[pl.ds(pl.multiple_of(r * MOE_ROWS, MOE_ROWS), MOE_ROWS)]
            return pltpu.make_async_copy(y_s.at[0:MOE_ROWS], dst, sem_out)

        def start(r, c):
            tail_copy(r).start()
            return c

        def wait(r, c):
            tail_copy(r).wait()
            return c

        lax.fori_loop(n_used_ref[0], y_hbm.shape[0] // MOE_ROWS, start, 0)
        lax.fori_loop(n_used_ref[0], y_hbm.shape[0] // MOE_ROWS, wait, 0)


def _moe_ffn(xs, plan, w_gate, b_gate, w_up, b_up, w_down, b_down, l, n_sb):
    sb_e, sb_blk0, sb_nblk, n_used = plan
    n_slots, D = xs.shape
    L, E, _, F = w_gate.shape
    tf = 256
    rows = SB_BLOCKS * MOE_ROWS
    n_f = F // tf

    def ftile(s, f, n):
        return jnp.where(n[s] > 0, f, n_f - 1)

    grid_spec = pltpu.PrefetchScalarGridSpec(
        num_scalar_prefetch=4,
        grid=(n_sb, n_f),
        in_specs=[
            pl.BlockSpec(memory_space=pl.ANY),
            pl.BlockSpec((1, 1, D, tf), lambda s, f, e, b, n, u: (l, e[s], 0, ftile(s, f, n))),
            pl.BlockSpec((1, 1, 1, tf), lambda s, f, e, b, n, u: (l, e[s], 0, ftile(s, f, n))),
            pl.BlockSpec((1, 1, D, tf), lambda s, f, e, b, n, u: (l, e[s], 0, ftile(s, f, n))),
            pl.BlockSpec((1, 1, 1, tf), lambda s, f, e, b, n, u: (l, e[s], 0, ftile(s, f, n))),
            pl.BlockSpec((1, 1, tf, D), lambda s, f, e, b, n, u: (l, e[s], ftile(s, f, n), 0)),
            pl.BlockSpec((1, 1, 1, D), lambda s, f, e, b, n, u: (l, e[s], 0, 0)),
        ],
        out_specs=pl.BlockSpec(memory_space=pl.ANY),
        scratch_shapes=[
            pltpu.VMEM((rows, D), BF16),
            pltpu.VMEM((rows, D), F32),
            pltpu.VMEM((XS_STAGES, MOE_ROWS, D), F32),
            pltpu.SemaphoreType.DMA((XS_STAGES,)),
            pltpu.SemaphoreType.DMA,
        ],
    )
    return pl.pallas_call(
        _moe_kernel,
        out_shape=jax.ShapeDtypeStruct((n_slots, D), F32),
        grid_spec=grid_spec,
        compiler_params=_cparams(("arbitrary", "arbitrary")),
        name="moe_ffn",
    )(sb_e, sb_blk0, sb_nblk, n_used, xs, w_gate, b_gate.reshape(L, E, 1, F), w_up, b_up.reshape(L, E, 1, F),
      w_down, b_down.reshape(L, E, 1, D))


def _combine_kernel(pos_ref, pos_next_ref, y_hbm, gates_ref, x_ref, mods_ref, g_ref, o_c_ref, o_s_ref, stage, sem,
                    *, final_norm, n_ctx_tiles):
    i = pl.program_id(0)
    n_steps = pl.num_programs(0)
    n = x_ref.shape[0]
    slot = i & 1

    def fetch(p_ref, sl):
        def issue(t, c):
            for k in range(TOP_K):
                p = p_ref[0, 0, t * TOP_K + k]
                pltpu.make_async_copy(y_hbm.at[pl.ds(p, 1)], stage.at[sl, k, pl.ds(t, 1)],
                                      sem.at[sl]).start(priority=k % 2)
            return c
        lax.fori_loop(0, n, issue, 0, unroll=2)

    @pl.when(i == 0)
    def _():
        fetch(pos_ref, slot)

    @pl.when(i + 1 < n_steps)
    def _():
        fetch(pos_next_ref, 1 - slot)

    for k in range(TOP_K):
        pltpu.make_async_copy(stage.at[slot, k], stage.at[slot, k], sem.at[slot]).wait()
    gates = gates_ref[...]
    acc = gates[:, 0:1] * stage[slot, 0]
    for k in range(1, TOP_K):
        acc = acc + gates[:, k:k + 1] * stage[slot, k]
    res = x_ref[...] + mods_ref[0][5:6] * acc
    res = _rms_lanes(res, g_ref[...]) if final_norm else res

    @pl.when(i < n_ctx_tiles)
    def _():
        o_c_ref[...] = res

    @pl.when(i >= n_ctx_tiles)
    def _():
        o_s_ref[...] = res


def _combine(y_slots, pos, gates, x, mods, g_final, final_norm, n_ctx_rows, dec_seq):
    T, D = x.shape
    cb = 128
    n_steps = T // cb
    nct = n_ctx_rows // cb
    return pl.pallas_call(
        functools.partial(_combine_kernel, final_norm=final_norm, n_ctx_tiles=nct),
        out_shape=(jax.ShapeDtypeStruct((n_ctx_rows, D), F32), jax.ShapeDtypeStruct((T - n_ctx_rows, D), F32)),
        grid=(n_steps,),
        in_specs=[
            pl.BlockSpec((1, 1, cb * TOP_K), lambda i: (i, 0, 0), memory_space=pltpu.SMEM),
            pl.BlockSpec((1, 1, cb * TOP_K), lambda i: (jnp.minimum(i + 1, n_steps - 1), 0, 0),
                         memory_space=pltpu.SMEM),
            pl.BlockSpec(memory_space=pl.ANY),
            pl.BlockSpec((cb, LANES), lambda i: (i, 0)),
            pl.BlockSpec((cb, D), lambda i: (i, 0)),
            pl.BlockSpec((1, N_MOD, D), lambda i: (_cond_index(i, cb, n_ctx_rows, dec_seq), 0, 0)),
            pl.BlockSpec((1, D), lambda i: (0, 0)),
        ],
        out_specs=(pl.BlockSpec((cb, D), lambda i: (jnp.minimum(i, nct - 1), 0)),
                   pl.BlockSpec((cb, D), lambda i: (jnp.maximum(i - nct, 0), 0))),
        scratch_shapes=[pltpu.VMEM((2, TOP_K, cb, D), F32), pltpu.SemaphoreType.DMA((2,))],
        compiler_params=_cparams(("arbitrary",)),
        name="moe_combine",
    )(pos.reshape(n_steps, 1, cb * TOP_K), pos.reshape(n_steps, 1, cb * TOP_K), y_slots, gates, x, mods, g_final)


def _rope_tables(n, head_dim):
    quarter = head_dim // 4
    inv = ROPE_THETA ** (-np.arange(quarter, dtype=np.float64) / quarter)
    pos = np.arange(n)
    row = (pos // GRID_W)[:, None] * inv
    col = (pos % GRID_W)[:, None] * inv
    cos = np.concatenate([np.cos(row), np.cos(row), np.cos(col), np.cos(col)], axis=1)
    sin = np.concatenate([-np.sin(row), np.sin(row), -np.sin(col), np.sin(col)], axis=1)
    reps = LANES // head_dim
    return (jnp.asarray(np.tile(cos, (1, reps)), dtype=F32), jnp.asarray(np.tile(sin, (1, reps)), dtype=F32))


def kernel(x_prompt, x_sample, c, cache_k_a, cache_v_a, cache_k_c, cache_v_c, c_ctx, w_mod, b_mod, g_norm1, g_norm2, w_in, lam_q1, lam_k1, lam_q2, lam_k2, g_subln_a, g_q_c, g_k_c, g_v_b, w_s_b, b_s_b, w_d, s_d, w_out, w_router, b_router, w_gate, b_gate, w_up, b_up, w_down, b_down, g_final):
    B, S, D = x_prompt.shape
    DB, DS, _ = x_sample.shape
    L = w_mod.shape[0]
    P = cache_k_a.shape[2]
    E = w_router.shape[2]
    n_ctx = B * S
    T = n_ctx + DB * DS
    GW = w_out.shape[1] // 4
    dk_a = cache_k_a.shape[-1]
    hd_c = cache_k_c.shape[-1]

    x_c = x_prompt.reshape(n_ctx, D)
    x_s = x_sample.reshape(DB * DS, D)
    cond8 = jnp.zeros((8, D), F32).at[0].set(c_ctx).at[1:1 + DB].set(c)
    mod = _modulation(cond8, w_mod, b_mod).reshape(L, 8, N_MOD, D)

    cos_a, sin_a = _rope_tables(DS, dk_a)
    cos_c, sin_c = _rope_tables(DS, hd_c)
    caches_all = (cache_k_a.reshape(DB, L, P, GW), cache_v_a.reshape(DB, L, P, GW),
                  cache_k_c.reshape(DB, L, P, GW // 2), cache_v_c.reshape(DB, L, P, GW // 2))

    w_in = _to_bf16(w_in)
    w_out = _to_bf16(w_out)
    n_slots_max = -(-(T * TOP_K + E * (MOE_ROWS - 1)) // MOE_ROWS)
    n_sb = E + n_slots_max // SB_BLOCKS

    states = []
    for l in range(L):
        mods = mod[l]
        lam0 = 0.8 - 0.6 * math.exp(-0.3 * l)
        g1 = g_norm1[l][None, :]
        proj_c = _inproj(x_c, 0, g1, mods, w_in, l, n_ctx, DS)
        proj_s = _inproj(x_s, n_ctx, g1, mods, w_in, l, n_ctx, DS)
        params = (jnp.stack([lam_q1[l], lam_k1[l], lam_q2[l], lam_k2[l]]), g_subln_a[l][None, :],
                  g_q_c[l][None, :], g_k_c[l][None, :], g_v_b[l][None, :], w_s_b[l], b_s_b[l].T,
                  w_d[l], s_d[l][None, :])
        mix_c, *state = _mixer(proj_c, l, B, S, S, lam0, params)
        mix_s = _mixer(proj_s, l, DB, DS, CHUNK, lam0, params, caches=caches_all,
                       rope_tabs=(cos_a, sin_a, cos_c, sin_c))
        states.append(state)
        x = _outproj(mix_c, mix_s, w_out, x_c, x_s, mods, l, n_ctx, DS)
        h, top_idx, gates = _router(x, g_norm2[l][None, :], mods, w_router[l], b_router[l], n_ctx, DS)
        pos, plan, pad_info = _route_plan(top_idx[:, :TOP_K], E, n_sb)
        xs = _dispatch(h, pos, pad_info, plan[3], n_slots_max * MOE_ROWS)
        y_slots = _moe_ffn(xs, plan, w_gate, b_gate, w_up, b_up, w_down, b_down, l, n_sb)
        x_c, x_s = _combine(y_slots, pos, gates, x, mods, g_final[None, :], l == L - 1, n_ctx, DS)

    y_prompt = x_c.reshape(B, S, D)
    y_sample = x_s.reshape(DB, DS, D)
    h_a = GW // (2 * dk_a)
    new_k_a = jnp.stack([st[0].reshape(B, S, h_a, 2, dk_a) for st in states], axis=1)
    new_v_a = jnp.stack([st[1].reshape(B, S, h_a, 2 * dk_a) for st in states], axis=1)
    new_k_c = jnp.stack([st[2].reshape(B, S, GW // 2 // hd_c, hd_c) for st in states], axis=1)
    new_v_c = jnp.stack([st[3].reshape(B, S, GW // 2 // hd_c, hd_c) for st in states], axis=1)
    return (y_prompt, y_sample, new_k_a, new_v_a, new_k_c, new_v_c)
```

```python
import functools
import math

import jax
import jax.numpy as jnp
import numpy as np
from jax import lax
from jax.experimental import pallas as pl
from jax.experimental.pallas import tpu as pltpu

F32 = jnp.float32
BF16 = jnp.bfloat16

EPS = 1e-6
GRID_W = 64
ROPE_THETA = 10000.0
N_MOD = 6
TOP_K = 4
SWIGLU_LIMIT = 7.0
SWIGLU_ALPHA = 1.702
POOL_WINDOWS = (2, 4, 8, 16)

LANES = 128
CHUNK = 128
MOE_ROWS = 128
SB_BLOCKS = 16
XS_STAGES = 8
FFN_CHUNK_LOG2 = 3
FFN_CHUNK = 1 << FFN_CHUNK_LOG2
VMEM_LIMIT = 56 * 1024 * 1024


def _cparams(sem, vmem=VMEM_LIMIT):
    return pltpu.CompilerParams(dimension_semantics=sem, vmem_limit_bytes=vmem)


def _dot(a, b):
    return jnp.dot(a, b, preferred_element_type=F32)


def _dot_nt(a, b):
    return lax.dot_general(a, b, (((1,), (1,)), ((), ())), preferred_element_type=F32)


def _rms_lanes(x, g):
    ms = jnp.mean(x * x, axis=-1, keepdims=True)
    return x * lax.rsqrt(ms + EPS) * g


def _mod_kernel(cond_ref, w_ref, b_ref, o_ref):
    c = cond_ref[...]
    a = (c * jax.nn.sigmoid(c)).astype(BF16)
    o_ref[0] = _dot(a, w_ref[0].astype(BF16)) + b_ref[0]


def _modulation(cond8, w_mod, b_mod):
    L, D, NM = w_mod.shape
    tn = 1024
    return pl.pallas_call(
        _mod_kernel,
        out_shape=jax.ShapeDtypeStruct((L, 8, NM), F32),
        grid=(L, NM // tn),
        in_specs=[
            pl.BlockSpec((8, D), lambda l, n: (0, 0)),
            pl.BlockSpec((1, D, tn), lambda l, n: (l, 0, n)),
            pl.BlockSpec((1, 1, tn), lambda l, n: (l, 0, n)),
        ],
        out_specs=pl.BlockSpec((1, 8, tn), lambda l, n: (l, 0, n)),
        compiler_params=_cparams(("arbitrary", "arbitrary")),
        name="modulation",
    )(cond8, w_mod, b_mod.reshape(L, 1, NM))


def _cast_kernel(w_ref, o_ref):
    o_ref[...] = w_ref[...].astype(BF16)


def _to_bf16(w):
    L, K, N = w.shape
    tn = 512
    return pl.pallas_call(
        _cast_kernel,
        out_shape=jax.ShapeDtypeStruct(w.shape, BF16),
        grid=(L, N // tn),
        in_specs=[pl.BlockSpec((1, K, tn), lambda l, n: (l, 0, n))],
        out_specs=pl.BlockSpec((1, K, tn), lambda l, n: (l, 0, n)),
        compiler_params=_cparams(("arbitrary", "arbitrary")),
        name="weights_to_bf16",
    )(w)


def _cond_index(row_block, rows_per_block, n_ctx_rows, dec_seq):
    r0 = row_block * rows_per_block
    return jnp.where(r0 < n_ctx_rows, 0, 1 + (r0 - n_ctx_rows) // dec_seq)


def _inproj_kernel(x_ref, g_ref, mods_ref, w_ref, o_ref, h_ref, *, shift_row, scale_row, chunk):
    @pl.when(pl.program_id(1) == 0)
    def _():
        m = mods_ref[0]
        shift = m[shift_row:shift_row + 1]
        scale1 = 1.0 + m[scale_row:scale_row + 1]
        g = g_ref[...]

        def body(i, carry):
            r = pl.multiple_of(i * chunk, chunk)
            x = x_ref[pl.ds(r, chunk), :]
            h_ref[pl.ds(r, chunk), :] = (_rms_lanes(x, g) * scale1 + shift).astype(BF16)
            return carry

        lax.fori_loop(0, x_ref.shape[0] // chunk, body, 0)

    o_ref[...] = _dot(h_ref[...], w_ref[0])


def _inproj(x, token_row0, g, mods, w_in, l, n_ctx_rows, dec_seq):
    n_rows, D = x.shape
    PW = w_in.shape[2]
    tm, tn = 1024, 1024
    tb0 = token_row0 // tm
    kern = functools.partial(_inproj_kernel, shift_row=0, scale_row=1, chunk=128)
    return pl.pallas_call(
        kern,
        out_shape=jax.ShapeDtypeStruct((n_rows, PW), F32),
        grid=(n_rows // tm, PW // tn),
        in_specs=[
            pl.BlockSpec((tm, D), lambda m, n: (m, 0)),
            pl.BlockSpec((1, D), lambda m, n: (0, 0)),
            pl.BlockSpec((1, N_MOD, D), lambda m, n: (_cond_index(tb0 + m, tm, n_ctx_rows, dec_seq), 0, 0)),
            pl.BlockSpec((1, D, tn), lambda m, n: (l, 0, n)),
        ],
        out_specs=pl.BlockSpec((tm, tn), lambda m, n: (m, n)),
        scratch_shapes=[pltpu.VMEM((tm, D), BF16)],
        compiler_params=_cparams(("arbitrary", "arbitrary")),
        name="inproj",
    )(x, g, mods, w_in)


def _rope(x, cos, sin, q):
    lane = lax.broadcasted_iota(jnp.int32, x.shape, 1)
    first_half = (lane & q) == 0
    partner = jnp.where(first_half, pltpu.roll(x, LANES - q, 1), pltpu.roll(x, q, 1))
    return x * cos + partner * sin


def _exp_scores(s):
    return jnp.exp(s - jnp.max(s, axis=-1, keepdims=True)).astype(BF16)


def _mixer_kernel(*refs, seq, qb, n_cache, lam0, rope):
    it = iter(refs)
    qa_ref, ub_ref, vb_ref, qc_ref = next(it), next(it), next(it), next(it)
    ka_ref, va_ref, kv_ref, pd_ref = next(it), next(it), next(it), next(it)
    if n_cache:
        cka_ref, cva_ref, ckc_ref, cvc_ref = next(it), next(it), next(it), next(it)
    if rope:
        cos_a_ref, sin_a_ref, cos_c_ref, sin_c_ref = next(it), next(it), next(it), next(it)
    lamp_ref, gsub_ref, gq_ref, gk_ref, gvb_ref = next(it), next(it), next(it), next(it), next(it)
    ws_ref, bst_ref, wd_ref, sd_ref = next(it), next(it), next(it), next(it)
    mix_ref = next(it)
    kas_ref, vas_ref, kcn_ref, vcs_ref = (None,) * 4 if n_cache else (next(it), next(it), next(it), next(it))
    ka_s, va_s, kc_s, vc_s = next(it), next(it), next(it), next(it)

    j = pl.program_id(1)
    row0 = pl.multiple_of(j * qb, qb)
    n_heads = qa_ref.shape[1] // LANES
    n_kv = kc_s.shape[1] // LANES
    dk_a = LANES // 2

    @pl.when(j == 0)
    def _():
        prep = 256

        def put_values(dst, rows, b, v):
            dst[rows, 2 * b * LANES:(2 * b + 1) * LANES] = v.astype(BF16)
            dst[rows, (2 * b + 1) * LANES:(2 * b + 2) * LANES] = jnp.ones(v.shape, BF16)

        for c in range(seq // prep):
            rows = slice(c * prep, (c + 1) * prep)
            for b in range(n_heads):
                cols = slice(b * LANES, (b + 1) * LANES)
                k = ka_ref[rows, cols]
                if rope:
                    k = _rope(k, cos_a_ref[rows, :], sin_a_ref[rows, :], dk_a // 4)
                ka_s[rows, cols] = k.astype(BF16)
                v = va_ref[rows, cols]
                put_values(va_s, rows, b, v)
                if kas_ref is not None:
                    kas_ref[rows, cols] = k
                    vas_ref[rows, cols] = v
            for b in range(n_kv):
                cols = slice(b * LANES, (b + 1) * LANES)
                k = _rms_lanes(kv_ref[rows, cols], gk_ref[...])
                v = kv_ref[rows, (n_kv + b) * LANES:(n_kv + b + 1) * LANES]
                if kcn_ref is not None:
                    kcn_ref[rows, cols] = k
                    vcs_ref[rows, cols] = v
                if rope:
                    k = _rope(k, cos_c_ref[rows, :], sin_c_ref[rows, :], LANES // 4)
                kc_s[rows, cols] = k.astype(BF16)
                put_values(vc_s, rows, b, v)
        if n_cache:
            tail = slice(seq, seq + n_cache)
            ka_s[tail, :] = cka_ref[0].astype(BF16)
            kc_s[tail, :] = ckc_ref[0].astype(BF16)
            for b in range(n_heads):
                put_values(va_s, tail, b, cva_ref[0, :, b * LANES:(b + 1) * LANES])
            for b in range(n_kv):
                put_values(vc_s, tail, b, cvc_ref[0, :, b * LANES:(b + 1) * LANES])

    lane = lax.broadcasted_iota(jnp.int32, (qb, LANES), 1)

    lp = lamp_ref[...]
    lam = (jnp.exp(jnp.sum(lp[0:1] * lp[1:2], axis=-1, keepdims=True))
           - jnp.exp(jnp.sum(lp[2:3] * lp[3:4], axis=-1, keepdims=True)) + lam0)
    scale_a = dk_a ** -0.5
    if rope:
        cos_a = cos_a_ref[pl.ds(row0, qb), :]
        sin_a = sin_a_ref[pl.ds(row0, qb), :]
        cos_c = cos_c_ref[pl.ds(row0, qb), :]
        sin_c = sin_c_ref[pl.ds(row0, qb), :]
    for h in range(n_heads):
        cols = slice(h * LANES, (h + 1) * LANES)
        q = qa_ref[:, cols]
        if rope:
            q = _rope(q, cos_a, sin_a, dk_a // 4)
        q = q * scale_a
        k = ka_s[:, cols]
        v1 = va_s[:, 2 * h * LANES:(2 * h + 2) * LANES]
        q0 = jnp.where(lane < dk_a, q, 0.0).astype(BF16)
        q1 = jnp.where(lane >= dk_a, q, 0.0).astype(BF16)
        o0 = _dot(_exp_scores(_dot_nt(q0, k)), v1)
        o1 = _dot(_exp_scores(_dot_nt(q1, k)), v1)
        o = o0[:, :LANES] * (1.0 / o0[:, LANES:LANES + 1]) - o1[:, :LANES] * (lam / o1[:, LANES:LANES + 1])
        mix_ref[:, cols] = (_rms_lanes(o, gsub_ref[...]) * (1.0 - lam0)).astype(BF16)

    off_b = n_heads * LANES
    for g in range(ub_ref.shape[1] // LANES):
        cols = slice(g * LANES, (g + 1) * LANES)
        vn = _rms_lanes(vb_ref[:, cols], gvb_ref[:, cols]).astype(BF16)
        w_s = ws_ref[g].astype(BF16)
        for c in range(qb // CHUNK):
            rows = slice(c * CHUNK, (c + 1) * CHUNK)
            mixed = _dot(w_s, vn[rows, :]) + bst_ref[:, g:g + 1]
            mix_ref[rows, off_b + g * LANES: off_b + (g + 1) * LANES] = (ub_ref[rows, cols] * mixed).astype(BF16)

    off_c = off_b + ub_ref.shape[1]
    scale_c = LANES ** -0.5
    for h in range(n_heads):
        cols = slice(h * LANES, (h + 1) * LANES)
        q = _rms_lanes(qc_ref[:, cols], gq_ref[...])
        if rope:
            q = _rope(q, cos_c, sin_c, LANES // 4)
        kv = h // (n_heads // n_kv)
        e = _exp_scores(_dot_nt((q * scale_c).astype(BF16), kc_s[:, kv * LANES:(kv + 1) * LANES]))
        o = _dot(e, vc_s[:, 2 * kv * LANES:(2 * kv + 2) * LANES])
        mix_ref[:, off_c + h * LANES: off_c + (h + 1) * LANES] = (
            o[:, :LANES] * (1.0 / o[:, LANES:LANES + 1])).astype(BF16)

    off_d = off_c + qc_ref.shape[1]
    win = min(qb + 2 * CHUNK, seq)
    start = pl.multiple_of(jnp.clip(row0 - CHUNK, 0, seq - win), CHUNK)
    t = row0 + lax.broadcasted_iota(jnp.int32, (qb, win), 0)
    col = start + lax.broadcasted_iota(jnp.int32, (qb, win), 1)
    t1 = row0 + lax.broadcasted_iota(jnp.int32, (qb, 1), 0)
    for g, w in enumerate(POOL_WINDOWS):
        cols = slice(g * LANES, (g + 1) * LANES)
        left, right = w // 2, w - 1 - w // 2
        band = jnp.where((col >= t - left) & (col <= t + right), 1.0, 0.0).astype(BF16)
        cnt = (jnp.minimum(t1 + right + 1, seq) - jnp.maximum(t1 - left, 0)).astype(F32)
        xw = pd_ref[pl.ds(start, win), cols]
        hi = xw.astype(BF16)
        lo = (xw - hi.astype(F32)).astype(BF16)
        wsum = _dot(band, hi) + _dot(band, lo)
        xg = pd_ref[pl.ds(row0, qb), cols]
        pooled = wsum / cnt - xg
        y = _dot(pooled.astype(BF16), wd_ref[g].astype(BF16)) * sd_ref[:, cols]
        mix_ref[:, off_d + g * LANES: off_d + (g + 1) * LANES] = y.astype(BF16)


def _mixer(proj, l, n_seq, seq, qb, lam0, params, caches=None, rope_tabs=None):
    T, PW = proj.shape
    GW = PW // 8
    n_qb = seq // qb
    n_cache = 0 if caches is None else caches[0].shape[2]
    n_keys = seq + n_cache

    def qspec(col):
        return pl.BlockSpec((qb, GW), lambda s, j: (s * n_qb + j, col))

    def sspec(col):
        return pl.BlockSpec((seq, GW), lambda s, j: (s, col))

    def full(shape):
        nd = len(shape)
        return pl.BlockSpec(shape, lambda s, j: (0,) * nd)

    in_specs = [qspec(0), qspec(3), qspec(4), qspec(5), sspec(1), sspec(2), sspec(6), sspec(7)]
    args = [proj] * 8
    if caches is not None:
        for c in caches:
            in_specs.append(pl.BlockSpec((None, 1, c.shape[2], c.shape[3]), lambda s, j: (s, l, 0, 0)))
            args.append(c)
    if rope_tabs is not None:
        for tab in rope_tabs:
            in_specs.append(full(tab.shape))
            args.append(tab)
    for p in params:
        in_specs.append(full(p.shape))
        args.append(p)

    MW = 4 * GW
    mix_spec = pl.BlockSpec((qb, MW), lambda s, j: (s * n_qb + j, 0))
    if caches is None:
        state_w = (GW, GW, GW // 2, GW // 2)
        out_shape = (jax.ShapeDtypeStruct((n_seq * seq, MW), BF16),
                     *(jax.ShapeDtypeStruct((n_seq * seq, w), F32) for w in state_w))
        out_specs = (mix_spec, *(pl.BlockSpec((seq, w), lambda s, j: (s, 0)) for w in state_w))
    else:
        out_shape = jax.ShapeDtypeStruct((n_seq * seq, MW), BF16)
        out_specs = mix_spec

    kern = functools.partial(_mixer_kernel, seq=seq, qb=qb, n_cache=n_cache, lam0=lam0,
                             rope=rope_tabs is not None)
    return pl.pallas_call(
        kern,
        out_shape=out_shape,
        grid=(n_seq, n_qb),
        in_specs=in_specs,
        out_specs=out_specs,
        scratch_shapes=[pltpu.VMEM((n_keys, GW), BF16), pltpu.VMEM((n_keys, 2 * GW), BF16),
                        pltpu.VMEM((n_keys, GW // 2), BF16), pltpu.VMEM((n_keys, GW), BF16)],
        compiler_params=_cparams(("arbitrary", "arbitrary")),
        name="mixer_latent" if caches is not None else "mixer_context",
    )(*args)


def _outproj_kernel(mix_c_ref, mix_s_ref, w_ref, x_c_ref, x_s_ref, mods_ref, o_ref, *, gate_row, n_ctx_tiles):
    tn = o_ref.shape[1]
    gcols = pl.ds(pl.multiple_of(pl.program_id(1) * tn, tn), tn)
    gate = mods_ref[0, gate_row:gate_row + 1, gcols]
    w = w_ref[0]
    is_ctx = pl.program_id(0) < n_ctx_tiles

    @pl.when(is_ctx)
    def _():
        o_ref[...] = x_c_ref[...] + gate * _dot(mix_c_ref[...], w)

    @pl.when(jnp.logical_not(is_ctx))
    def _():
        o_ref[...] = x_s_ref[...] + gate * _dot(mix_s_ref[...], w)


def _outproj(mix_c, mix_s, w_out, x_c, x_s, mods, l, n_ctx_rows, dec_seq):
    T = mix_c.shape[0] + mix_s.shape[0]
    D = x_c.shape[1]
    MW = mix_c.shape[1]
    tm, tn = 1024, 512
    nct = n_ctx_rows // tm
    kern = functools.partial(_outproj_kernel, gate_row=2, n_ctx_tiles=nct)
    return pl.pallas_call(
        kern,
        out_shape=jax.ShapeDtypeStruct((T, D), F32),
        grid=(T // tm, D // tn),
        in_specs=[
            pl.BlockSpec((tm, MW), lambda m, n: (jnp.minimum(m, nct - 1), 0)),
            pl.BlockSpec((tm, MW), lambda m, n: (jnp.maximum(m - nct, 0), 0)),
            pl.BlockSpec((1, MW, tn), lambda m, n: (l, 0, n)),
            pl.BlockSpec((tm, tn), lambda m, n: (jnp.minimum(m, nct - 1), jnp.where(m < nct, n, D // tn - 1))),
            pl.BlockSpec((tm, tn), lambda m, n: (jnp.maximum(m - nct, 0), jnp.where(m < nct, 0, n))),
            pl.BlockSpec((1, N_MOD, D), lambda m, n: (_cond_index(m, tm, n_ctx_rows, dec_seq), 0, 0)),
        ],
        out_specs=pl.BlockSpec((tm, tn), lambda m, n: (m, n)),
        compiler_params=_cparams(("arbitrary", "arbitrary")),
        name="outproj",
    )(mix_c, mix_s, w_out, x_c, x_s, mods)


def _split_bf16(x):
    hi = x.astype(BF16)
    return hi, (x - hi.astype(F32)).astype(BF16)


def _router_kernel(x_ref, g_ref, mods_ref, wr_ref, br_ref, h_ref, idx_ref, gate_ref, *, n_experts):
    m = mods_ref[0]
    h = _rms_lanes(x_ref[...], g_ref[...]) * (1.0 + m[4:5]) + m[3:4]
    h_ref[...] = h
    h_hi, h_lo = _split_bf16(h)
    w_hi, w_lo = _split_bf16(wr_ref[...])
    logits = _dot(h_hi, w_hi) + _dot(h_hi, w_lo) + _dot(h_lo, w_hi) + br_ref[...]
    lane = lax.broadcasted_iota(jnp.int32, logits.shape, 1)
    lane_f = lane.astype(F32)
    neg = jnp.float32(-jnp.inf)
    logits = jnp.where(lane < n_experts, logits, neg)
    idx_out = jnp.zeros(logits.shape, F32)
    val_out = jnp.zeros(logits.shape, F32)
    top = None
    denom = None
    for k in range(TOP_K):
        v = jnp.max(logits, axis=-1, keepdims=True)
        i = jnp.min(jnp.where(logits == v, lane_f, float(LANES)), axis=-1, keepdims=True)
        if k == 0:
            top = v
        e = jnp.exp(v - top)
        denom = e if denom is None else denom + e
        idx_out = jnp.where(lane == k, i, idx_out)
        val_out = jnp.where(lane == k, e, val_out)
        logits = jnp.where(lane_f == i, neg, logits)
    idx_ref[...] = idx_out.astype(jnp.int32)
    gate_ref[...] = val_out / denom


def _router(x, g, mods, w_router_l, b_router_l, n_ctx_rows, dec_seq):
    T, D = x.shape
    E = w_router_l.shape[1]
    tm = 512
    wr = jnp.zeros((D, LANES), F32).at[:, :E].set(w_router_l)
    br = jnp.zeros((1, LANES), F32).at[0, :E].set(b_router_l)
    kern = functools.partial(_router_kernel, n_experts=E)
    return pl.pallas_call(
        kern,
        out_shape=(jax.ShapeDtypeStruct((T, D), F32),
                   jax.ShapeDtypeStruct((T, LANES), jnp.int32),
                   jax.ShapeDtypeStruct((T, LANES), F32)),
        grid=(T // tm,),
        in_specs=[
            pl.BlockSpec((tm, D), lambda m: (m, 0)),
            pl.BlockSpec((1, D), lambda m: (0, 0)),
            pl.BlockSpec((1, N_MOD, D), lambda m: (_cond_index(m, tm, n_ctx_rows, dec_seq), 0, 0)),
            pl.BlockSpec((D, LANES), lambda m: (0, 0)),
            pl.BlockSpec((1, LANES), lambda m: (0, 0)),
        ],
        out_specs=(pl.BlockSpec((tm, D), lambda m: (m, 0)),
                   pl.BlockSpec((tm, LANES), lambda m: (m, 0)),
                   pl.BlockSpec((tm, LANES), lambda m: (m, 0))),
        compiler_params=_cparams(("arbitrary",)),
        name="norm_router",
    )(x, g, mods, wr, br)


def _route_plan(top_idx, n_experts, n_sb):
    T, K = top_idx.shape
    flat_e = top_idx.reshape(-1)
    onehot = (flat_e[:, None] == jnp.arange(n_experts, dtype=jnp.int32)[None, :]).astype(jnp.int32)
    csum = jnp.cumsum(onehot, axis=0)
    counts = csum[-1]
    nblk = (counts + MOE_ROWS - 1) // MOE_ROWS
    blk_start = jnp.cumsum(nblk) - nblk
    pos = jnp.sum(onehot * (csum - 1 + (blk_start * MOE_ROWS)[None, :]), axis=1)
    nsb = (nblk + SB_BLOCKS - 1) // SB_BLOCKS
    sb_end = jnp.cumsum(nsb)
    s = jnp.arange(n_sb, dtype=jnp.int32)
    total = sb_end[-1]
    sc = jnp.minimum(s, total - 1)
    e = jnp.sum((sb_end[None, :] <= sc[:, None]).astype(jnp.int32), axis=1)
    pick = (e[:, None] == jnp.arange(n_experts, dtype=jnp.int32)[None, :]).astype(jnp.int32)

    def of_expert(v):
        return jnp.sum(pick * v[None, :], axis=1)

    k = sc - (of_expert(sb_end) - of_expert(nsb))
    sb_blk0 = (of_expert(blk_start) + k * SB_BLOCKS).astype(jnp.int32)
    sb_nblk = jnp.where(s < total, jnp.minimum(SB_BLOCKS, of_expert(nblk) - k * SB_BLOCKS), 0).astype(jnp.int32)
    n_used = jnp.sum(nblk).astype(jnp.int32).reshape(1)
    plan = (e, sb_blk0, sb_nblk, n_used)
    return pos.reshape(T, K).astype(jnp.int32), plan, (blk_start.astype(jnp.int32), counts.astype(jnp.int32))


def _dispatch_kernel(start_ref, cnt_ref, n_used_ref, pos_ref, h_ref, xs_hbm, ring, zeros, sem, zsem):
    i = pl.program_id(0)
    n = pl.num_programs(0)
    tt = h_ref.shape[0]
    slot = i & 1

    ring[slot] = h_ref[...]

    def issue(t, c):
        for k in range(TOP_K):
            p = pos_ref[0, 0, t * TOP_K + k]
            pltpu.make_async_copy(ring.at[slot, pl.ds(t, 1)], xs_hbm.at[pl.ds(p, 1)],
                                  sem.at[slot]).start(priority=k % 2)
        return c

    lax.fori_loop(0, tt, issue, 0, unroll=2)

    def drain(sl):
        for _ in range(TOP_K):
            pltpu.make_async_copy(ring.at[sl], ring.at[sl], sem.at[sl]).wait()

    @pl.when(i == 0)
    def _():
        zeros[...] = jnp.zeros(zeros.shape, F32)

        def pad_copy(row):
            return pltpu.make_async_copy(zeros.at[pl.ds(0, 1)], xs_hbm.at[pl.ds(row, 1)], zsem)

        def tail_copy(b):
            dst = xs_hbm.at[pl.ds(pl.multiple_of(b * MOE_ROWS, MOE_ROWS), MOE_ROWS)]
            return pltpu.make_async_copy(zeros, dst, zsem)

        def per_expert(fn):
            def body(e, c):
                cnt = cnt_ref[e]
                first = start_ref[e] * MOE_ROWS + cnt
                lax.fori_loop(0, (-cnt) & (MOE_ROWS - 1), lambda j, cc: (fn(first + j), cc)[1], 0)
                return c
            lax.fori_loop(0, cnt_ref.shape[0], body, 0)

        def per_tail(fn):
            lax.fori_loop(n_used_ref[0], xs_hbm.shape[0] // MOE_ROWS, lambda b, cc: (fn(b), cc)[1], 0)

        per_expert(lambda row: pad_copy(row).start())
        per_tail(lambda b: tail_copy(b).start())
        per_expert(lambda row: pad_copy(row).wait())
        per_tail(lambda b: tail_copy(b).wait())

    @pl.when(i > 0)
    def _():
        drain(1 - slot)

    @pl.when(i == n - 1)
    def _():
        drain(slot)


def _dispatch(h, pos, pad_info, n_used, n_slots):
    T, D = h.shape
    blk_start, counts = pad_info
    tt = 256
    grid_spec = pltpu.PrefetchScalarGridSpec(
        num_scalar_prefetch=3,
        grid=(T // tt,),
        in_specs=[
            pl.BlockSpec((1, 1, tt * TOP_K), lambda i, s, c, u: (i, 0, 0), memory_space=pltpu.SMEM),
            pl.BlockSpec((tt, D), lambda i, s, c, u: (i, 0)),
        ],
        out_specs=pl.BlockSpec(memory_space=pl.ANY),
        scratch_shapes=[pltpu.VMEM((2, tt, D), F32), pltpu.VMEM((MOE_ROWS, D), F32),
                        pltpu.SemaphoreType.DMA((2,)), pltpu.SemaphoreType.DMA],
    )
    return pl.pallas_call(
        _dispatch_kernel,
        out_shape=jax.ShapeDtypeStruct((n_slots, D), F32),
        grid_spec=grid_spec,
        compiler_params=_cparams(("arbitrary",)),
        name="moe_dispatch",
    )(blk_start, counts, n_used, pos.reshape(T // tt, 1, tt * TOP_K), h)


def _moe_kernel(sb_e_ref, sb_blk0_ref, sb_nblk_ref, n_used_ref, xs_hbm, wg_ref, bg_ref, wu_ref, bu_ref,
                wd_ref, bd_ref, y_hbm, x_s, y_s, stage, sem_in, sem_out):
    s = pl.program_id(0)
    f = pl.program_id(1)
    n_s = pl.num_programs(0)
    n_f = pl.num_programs(1)
    nblk = sb_nblk_ref[s]
    blk0 = sb_blk0_ref[s]
    tf = wg_ref.shape[3]
    n_stage = stage.shape[0]

    def rows_of(blk, nb):
        return pl.ds(pl.multiple_of(blk * MOE_ROWS, MOE_ROWS), nb * MOE_ROWS)

    def in_copy(first_blk, r):
        return pltpu.make_async_copy(xs_hbm.at[rows_of(first_blk + r, 1)], stage.at[r & (n_stage - 1)],
                                     sem_in.at[r & (n_stage - 1)])

    def out_copy(first_blk, r):
        return pltpu.make_async_copy(y_s.at[rows_of(r, 1)], y_hbm.at[rows_of(first_blk + r, 1)], sem_out)

    def for_blocks(n, fn):
        def body(r, c):
            fn(r)
            return c
        lax.fori_loop(0, n, body, 0)

    def prefetch(first_blk, n):
        for_blocks(jnp.minimum(n, n_stage), lambda r: in_copy(first_blk, r).start())

    def ffn(blk, nb, first):
        rows = rows_of(blk, nb)
        if first:
            for i in range(nb):
                r = blk + i
                in_copy(blk0, r).wait()
                x_s[rows_of(r, 1), :] = stage[r & (n_stage - 1)].astype(BF16)

                @pl.when(r + n_stage < nblk)
                def _(r=r):
                    in_copy(blk0, r + n_stage).start()

        wgu = jnp.concatenate([wg_ref[0, 0].astype(BF16), wu_ref[0, 0].astype(BF16)], axis=1)
        gu = _dot(x_s[rows, :], wgu)
        g = jnp.minimum(gu[:, :tf] + bg_ref[0, 0], SWIGLU_LIMIT)
        u = jnp.clip(gu[:, tf:] + bu_ref[0, 0], -SWIGLU_LIMIT, SWIGLU_LIMIT)
        act = (u + 1.0) * g * jax.nn.sigmoid(SWIGLU_ALPHA * g)
        y = _dot(act.astype(BF16), wd_ref[0, 0].astype(BF16))
        if first:
            y_s[rows, :] = y + bd_ref[0, 0]
        else:
            y_s[rows, :] += y

    def all_chunks(first):
        n_big = lax.shift_right_logical(nblk, FFN_CHUNK_LOG2)
        for_blocks(n_big, lambda c: ffn(c * FFN_CHUNK, FFN_CHUNK, first))
        done = n_big * FFN_CHUNK
        nb = FFN_CHUNK // 2
        while nb >= 1:
            @pl.when((nblk & nb) != 0)
            def _(done=done, nb=nb):
                ffn(done, nb, first)
            done = done + (nblk & nb)
            nb //= 2

    @pl.when(nblk > 0)
    def _():
        @pl.when(f == 0)
        def _():
            @pl.when(s == 0)
            def _():
                prefetch(blk0, nblk)

            @pl.when(s > 0)
            def _():
                for_blocks(sb_nblk_ref[s - 1], lambda r: out_copy(sb_blk0_ref[s - 1], r).wait())

            all_chunks(True)

        @pl.when(f > 0)
        def _():
            all_chunks(False)

        @pl.when(f == n_f - 1)
        def _():
            for_blocks(nblk, lambda r: out_copy(blk0, r).start())
            nxt = jnp.minimum(s + 1, n_s - 1)
            has_next = jnp.logical_and(s + 1 < n_s, sb_nblk_ref[nxt] > 0)

            @pl.when(has_next)
            def _():
                prefetch(sb_blk0_ref[nxt], sb_nblk_ref[nxt])

            @pl.when(jnp.logical_not(has_next))
            def _():
                for_blocks(nblk, lambda r: out_copy(blk0, r).wait())

    @pl.when(jnp.logical_and(s == pl.num_programs(0) - 1, f == n_f - 1))
    def _():
        y_s[0:MOE_ROWS, :] = jnp.zeros((MOE_ROWS, y_s.shape[1]), F32)

        def tail_copy(r):
            dst = y_hbm.at[pl.ds(pl.multiple_of(r * MOE_ROWS, MOE_ROWS), MOE_ROWS)]
            return pltpu.make_async_copy(y_s.at[0:MOE_ROWS], dst, sem_out)

        def start(r, c):
            tail_copy(r).start()
            return c

        def wait(r, c):
            tail_copy(r).wait()
            return c

        lax.fori_loop(n_used_ref[0], y_hbm.shape[0] // MOE_ROWS, start, 0)
        lax.fori_loop(n_used_ref[0], y_hbm.shape[0] // MOE_ROWS, wait, 0)


def _moe_ffn(xs, plan, w_gate, b_gate, w_up, b_up, w_down, b_down, l, n_sb):
    sb_e, sb_blk0, sb_nblk, n_used = plan
    n_slots, D = xs.shape
    L, E, _, F = w_gate.shape
    tf = 256
    rows = SB_BLOCKS * MOE_ROWS
    n_f = F // tf

    def ftile(s, f, n):
        return jnp.where(n[s] > 0, f, n_f - 1)

    grid_spec = pltpu.PrefetchScalarGridSpec(
        num_scalar_prefetch=4,
        grid=(n_sb, n_f),
        in_specs=[
            pl.BlockSpec(memory_space=pl.ANY),
            pl.BlockSpec((1, 1, D, tf), lambda s, f, e, b, n, u: (l, e[s], 0, ftile(s, f, n))),
            pl.BlockSpec((1, 1, 1, tf), lambda s, f, e, b, n, u: (l, e[s], 0, ftile(s, f, n))),
            pl.BlockSpec((1, 1, D, tf), lambda s, f, e, b, n, u: (l, e[s], 0, ftile(s, f, n))),
            pl.BlockSpec((1, 1, 1, tf), lambda s, f, e, b, n, u: (l, e[s], 0, ftile(s, f, n))),
            pl.BlockSpec((1, 1, tf, D), lambda s, f, e, b, n, u: (l, e[s], ftile(s, f, n), 0)),
            pl.BlockSpec((1, 1, 1, D), lambda s, f, e, b, n, u: (l, e[s], 0, 0)),
        ],
        out_specs=pl.BlockSpec(memory_space=pl.ANY),
        scratch_shapes=[
            pltpu.VMEM((rows, D), BF16),
            pltpu.VMEM((rows, D), F32),
            pltpu.VMEM((XS_STAGES, MOE_ROWS, D), F32),
            pltpu.SemaphoreType.DMA((XS_STAGES,)),
            pltpu.SemaphoreType.DMA,
        ],
    )
    return pl.pallas_call(
        _moe_kernel,
        out_shape=jax.ShapeDtypeStruct((n_slots, D), F32),
        grid_spec=grid_spec,
        compiler_params=_cparams(("arbitrary", "arbitrary")),
        name="moe_ffn",
    )(sb_e, sb_blk0, sb_nblk, n_used, xs, w_gate, b_gate.reshape(L, E, 1, F), w_up, b_up.reshape(L, E, 1, F),
      w_down, b_down.reshape(L, E, 1, D))


def _combine_kernel(pos_ref, pos_next_ref, y_hbm, gates_ref, x_ref, mods_ref, g_ref, o_c_ref, o_s_ref, stage, sem,
                    *, final_norm, n_ctx_tiles):
    i = pl.program_id(0)
    n_steps = pl.num_programs(0)
    n = x_ref.shape[0]
    slot = i & 1

    def fetch(p_ref, sl):
        def issue(t, c):
            for k in range(TOP_K):
                p = p_ref[0, 0, t * TOP_K + k]
                pltpu.make_async_copy(y_hbm.at[pl.ds(p, 1)], stage.at[sl, k, pl.ds(t, 1)],
                                      sem.at[sl]).start(priority=k % 2)
            return c
        lax.fori_loop(0, n, issue, 0, unroll=2)

    @pl.when(i == 0)
    def _():
        fetch(pos_ref, slot)

    @pl.when(i + 1 < n_steps)
    def _():
        fetch(pos_next_ref, 1 - slot)

    for k in range(TOP_K):
        pltpu.make_async_copy(stage.at[slot, k], stage.at[slot, k], sem.at[slot]).wait()
    gates = gates_ref[...]
    acc = gates[:, 0:1] * stage[slot, 0]
    for k in range(1, TOP_K):
        acc = acc + gates[:, k:k + 1] * stage[slot, k]
    res = x_ref[...] + mods_ref[0][5:6] * acc
    res = _rms_lanes(res, g_ref[...]) if final_norm else res

    @pl.when(i < n_ctx_tiles)
    def _():
        o_c_ref[...] = res

    @pl.when(i >= n_ctx_tiles)
    def _():
        o_s_ref[...] = res


def _combine(y_slots, pos, gates, x, mods, g_final, final_norm, n_ctx_rows, dec_seq):
    T, D = x.shape
    cb = 256
    n_steps = T // cb
    nct = n_ctx_rows // cb
    return pl.pallas_call(
        functools.partial(_combine_kernel, final_norm=final_norm, n_ctx_tiles=nct),
        out_shape=(jax.ShapeDtypeStruct((n_ctx_rows, D), F32), jax.ShapeDtypeStruct((T - n_ctx_rows, D), F32)),
        grid=(n_steps,),
        in_specs=[
            pl.BlockSpec((1, 1, cb * TOP_K), lambda i: (i, 0, 0), memory_space=pltpu.SMEM),
            pl.BlockSpec((1, 1, cb * TOP_K), lambda i: (jnp.minimum(i + 1, n_steps - 1), 0, 0),
                         memory_space=pltpu.SMEM),
            pl.BlockSpec(memory_space=pl.ANY),
            pl.BlockSpec((cb, LANES), lambda i: (i, 0)),
            pl.BlockSpec((cb, D), lambda i: (i, 0)),
            pl.BlockSpec((1, N_MOD, D), lambda i: (_cond_index(i, cb, n_ctx_rows, dec_seq), 0, 0)),
            pl.BlockSpec((1, D), lambda i: (0, 0)),
        ],
        out_specs=(pl.BlockSpec((cb, D), lambda i: (jnp.minimum(i, nct - 1), 0)),
                   pl.BlockSpec((cb, D), lambda i: (jnp.maximum(i - nct, 0), 0))),
        scratch_shapes=[pltpu.VMEM((2, TOP_K, cb, D), F32), pltpu.SemaphoreType.DMA((2,))],
        compiler_params=_cparams(("arbitrary",)),
        name="moe_combine",
    )(pos.reshape(n_steps, 1, cb * TOP_K), pos.reshape(n_steps, 1, cb * TOP_K), y_slots, gates, x, mods, g_final)


def _rope_tables(n, head_dim):
    quarter = head_dim // 4
    inv = ROPE_THETA ** (-np.arange(quarter, dtype=np.float64) / quarter)
    pos = np.arange(n)
    row = (pos // GRID_W)[:, None] * inv
    col = (pos % GRID_W)[:, None] * inv
    cos = np.concatenate([np.cos(row), np.cos(row), np.cos(col), np.cos(col)], axis=1)
    sin = np.concatenate([-np.sin(row), np.sin(row), -np.sin(col), np.sin(col)], axis=1)
    reps = LANES // head_dim
    return (jnp.asarray(np.tile(cos, (1, reps)), dtype=F32), jnp.asarray(np.tile(sin, (1, reps)), dtype=F32))


def kernel(x_prompt, x_sample, c, cache_k_a, cache_v_a, cache_k_c, cache_v_c, c_ctx, w_mod, b_mod, g_norm1, g_norm2, w_in, lam_q1, lam_k1, lam_q2, lam_k2, g_subln_a, g_q_c, g_k_c, g_v_b, w_s_b, b_s_b, w_d, s_d, w_out, w_router, b_router, w_gate, b_gate, w_up, b_up, w_down, b_down, g_final):
    B, S, D = x_prompt.shape
    DB, DS, _ = x_sample.shape
    L = w_mod.shape[0]
    P = cache_k_a.shape[2]
    E = w_router.shape[2]
    n_ctx = B * S
    T = n_ctx + DB * DS
    GW = w_out.shape[1] // 4
    dk_a = cache_k_a.shape[-1]
    hd_c = cache_k_c.shape[-1]

    x_c = x_prompt.reshape(n_ctx, D)
    x_s = x_sample.reshape(DB * DS, D)
    cond8 = jnp.zeros((8, D), F32).at[0].set(c_ctx).at[1:1 + DB].set(c)
    mod = _modulation(cond8, w_mod, b_mod).reshape(L, 8, N_MOD, D)

    cos_a, sin_a = _rope_tables(DS, dk_a)
    cos_c, sin_c = _rope_tables(DS, hd_c)
    caches_all = (cache_k_a.reshape(DB, L, P, GW), cache_v_a.reshape(DB, L, P, GW),
                  cache_k_c.reshape(DB, L, P, GW // 2), cache_v_c.reshape(DB, L, P, GW // 2))

    w_in = _to_bf16(w_in)
    w_out = _to_bf16(w_out)
    n_slots_max = -(-(T * TOP_K + E * (MOE_ROWS - 1)) // MOE_ROWS)
    n_sb = E + n_slots_max // SB_BLOCKS

    states = []
    for l in range(L):
        mods = mod[l]
        lam0 = 0.8 - 0.6 * math.exp(-0.3 * l)
        g1 = g_norm1[l][None, :]
        proj_c = _inproj(x_c, 0, g1, mods, w_in, l, n_ctx, DS)
        proj_s = _inproj(x_s, n_ctx, g1, mods, w_in, l, n_ctx, DS)
        params = (jnp.stack([lam_q1[l], lam_k1[l], lam_q2[l], lam_k2[l]]), g_subln_a[l][None, :],
                  g_q_c[l][None, :], g_k_c[l][None, :], g_v_b[l][None, :], w_s_b[l], b_s_b[l].T,
                  w_d[l], s_d[l][None, :])
        mix_c, *state = _mixer(proj_c, l, B, S, S, lam0, params)
        mix_s = _mixer(proj_s, l, DB, DS, 2 * CHUNK, lam0, params, caches=caches_all,
                       rope_tabs=(cos_a, sin_a, cos_c, sin_c))
        states.append(state)
        x = _outproj(mix_c, mix_s, w_out, x_c, x_s, mods, l, n_ctx, DS)
        h, top_idx, gates = _router(x, g_norm2[l][None, :], mods, w_router[l], b_router[l], n_ctx, DS)
        pos, plan, pad_info = _route_plan(top_idx[:, :TOP_K], E, n_sb)
        xs = _dispatch(h, pos, pad_info, plan[3], n_slots_max * MOE_ROWS)
        y_slots = _moe_ffn(xs, plan, w_gate, b_gate, w_up, b_up, w_down, b_down, l, n_sb)
        x_c, x_s = _combine(y_slots, pos, gates, x, mods, g_final[None, :], l == L - 1, n_ctx, DS)

    y_prompt = x_c.reshape(B, S, D)
    y_sample = x_s.reshape(DB, DS, D)
    h_a = GW // (2 * dk_a)
    new_k_a = jnp.stack([st[0].reshape(B, S, h_a, 2, dk_a) for st in states], axis=1)
    new_v_a = jnp.stack([st[1].reshape(B, S, h_a, 2 * dk_a) for st in states], axis=1)
    new_k_c = jnp.stack([st[2].reshape(B, S, GW // 2 // hd_c, hd_c) for st in states], axis=1)
    new_v_c = jnp.stack([st[3].reshape(B, S, GW // 2 // hd_c, hd_c) for st in states], axis=1)
    return (y_prompt, y_sample, new_k_a, new_v_a, new_k_c, new_v_c)
```

```python
import functools
import math

import jax
import jax.numpy as jnp
import numpy as np
from jax import lax
from jax.experimental import pallas as pl
from jax.experimental.pallas import tpu as pltpu

F32 = jnp.float32
BF16 = jnp.bfloat16

EPS = 1e-6
GRID_W = 64
ROPE_THETA = 10000.0
N_MOD = 6
TOP_K = 4
SWIGLU_LIMIT = 7.0
SWIGLU_ALPHA = 1.702
POOL_WINDOWS = (2, 4, 8, 16)

LANES = 128
CHUNK = 128
MOE_ROWS = 128
SB_BLOCKS = 16
XS_STAGES = 8
FFN_CHUNK_LOG2 = 3
FFN_CHUNK = 1 << FFN_CHUNK_LOG2

VMEM_LIMIT = 56 * 1024 * 1024
TM_DENSE = 1024
TN_INPROJ = 1024
TN_OUTPROJ = 512
TN_MOD = 1024
TN_CAST = 512
TM_ROUTER = 512
TF_MOE = 256
TT_DISPATCH = 256
CB_COMBINE = 256


def _cparams(sem, vmem=VMEM_LIMIT):
    return pltpu.CompilerParams(dimension_semantics=sem, vmem_limit_bytes=vmem)


def _dot(a, b):
    return jnp.dot(a, b, preferred_element_type=F32)


def _dot_nt(a, b):
    return lax.dot_general(a, b, (((1,), (1,)), ((), ())), preferred_element_type=F32)


def _rms_lanes(x, g):
    ms = jnp.mean(x * x, axis=-1, keepdims=True)
    return x * lax.rsqrt(ms + EPS) * g


def _mod_kernel(cond_ref, w_ref, b_ref, o_ref):
    c = cond_ref[...]
    a = (c * jax.nn.sigmoid(c)).astype(BF16)
    o_ref[0] = _dot(a, w_ref[0].astype(BF16)) + b_ref[0]


def _modulation(cond8, w_mod, b_mod):
    L, D, NM = w_mod.shape
    tn = TN_MOD
    return pl.pallas_call(
        _mod_kernel,
        out_shape=jax.ShapeDtypeStruct((L, 8, NM), F32),
        grid=(L, NM // tn),
        in_specs=[
            pl.BlockSpec((8, D), lambda l, n: (0, 0)),
            pl.BlockSpec((1, D, tn), lambda l, n: (l, 0, n)),
            pl.BlockSpec((1, 1, tn), lambda l, n: (l, 0, n)),
        ],
        out_specs=pl.BlockSpec((1, 8, tn), lambda l, n: (l, 0, n)),
        compiler_params=_cparams(("arbitrary", "arbitrary")),
        name="modulation",
    )(cond8, w_mod, b_mod.reshape(L, 1, NM))


def _cast_kernel(w_ref, o_ref):
    o_ref[...] = w_ref[...].astype(BF16)


def _to_bf16(w):
    L, K, N = w.shape
    tn = TN_CAST
    return pl.pallas_call(
        _cast_kernel,
        out_shape=jax.ShapeDtypeStruct(w.shape, BF16),
        grid=(L, N // tn),
        in_specs=[pl.BlockSpec((1, K, tn), lambda l, n: (l, 0, n))],
        out_specs=pl.BlockSpec((1, K, tn), lambda l, n: (l, 0, n)),
        compiler_params=_cparams(("arbitrary", "arbitrary")),
        name="weights_to_bf16",
    )(w)


def _cond_index(row_block, rows_per_block, n_ctx_rows, dec_seq):
    r0 = row_block * rows_per_block
    return jnp.where(r0 < n_ctx_rows, 0, 1 + (r0 - n_ctx_rows) // dec_seq)


def _inproj_kernel(x_ref, g_ref, mods_ref, w_ref, o_ref, h_ref, *, shift_row, scale_row, chunk):
    @pl.when(pl.program_id(1) == 0)
    def _():
        m = mods_ref[0]
        shift = m[shift_row:shift_row + 1]
        scale1 = 1.0 + m[scale_row:scale_row + 1]
        g = g_ref[...]

        def body(i, carry):
            r = pl.multiple_of(i * chunk, chunk)
            x = x_ref[pl.ds(r, chunk), :]
            h_ref[pl.ds(r, chunk), :] = (_rms_lanes(x, g) * scale1 + shift).astype(BF16)
            return carry

        lax.fori_loop(0, x_ref.shape[0] // chunk, body, 0)

    o_ref[...] = _dot(h_ref[...], w_ref[0])


def _inproj(x, token_row0, g, mods, w_in, l, n_ctx_rows, dec_seq):
    n_rows, D = x.shape
    PW = w_in.shape[2]
    tm, tn = TM_DENSE, TN_INPROJ
    tb0 = token_row0 // tm
    kern = functools.partial(_inproj_kernel, shift_row=0, scale_row=1, chunk=128)
    return pl.pallas_call(
        kern,
        out_shape=jax.ShapeDtypeStruct((n_rows, PW), F32),
        grid=(n_rows // tm, PW // tn),
        in_specs=[
            pl.BlockSpec((tm, D), lambda m, n: (m, 0)),
            pl.BlockSpec((1, D), lambda m, n: (0, 0)),
            pl.BlockSpec((1, N_MOD, D), lambda m, n: (_cond_index(tb0 + m, tm, n_ctx_rows, dec_seq), 0, 0)),
            pl.BlockSpec((1, D, tn), lambda m, n: (l, 0, n)),
        ],
        out_specs=pl.BlockSpec((tm, tn), lambda m, n: (m, n)),
        scratch_shapes=[pltpu.VMEM((tm, D), BF16)],
        compiler_params=_cparams(("arbitrary", "arbitrary")),
        name="inproj",
    )(x, g, mods, w_in)


def _rope(x, cos, sin, q):
    lane = lax.broadcasted_iota(jnp.int32, x.shape, 1)
    first_half = (lane & q) == 0
    partner = jnp.where(first_half, pltpu.roll(x, LANES - q, 1), pltpu.roll(x, q, 1))
    return x * cos + partner * sin


def _exp_scores(s):
    return jnp.exp(s - jnp.max(s, axis=-1, keepdims=True)).astype(BF16)


def _mixer_kernel(*refs, seq, qb, n_cache, lam0, rope):
    it = iter(refs)
    qa_ref, ub_ref, vb_ref, qc_ref = next(it), next(it), next(it), next(it)
    ka_ref, va_ref, kv_ref, pd_ref = next(it), next(it), next(it), next(it)
    if n_cache:
        cka_ref, cva_ref, ckc_ref, cvc_ref = next(it), next(it), next(it), next(it)
    if rope:
        cos_a_ref, sin_a_ref, cos_c_ref, sin_c_ref = next(it), next(it), next(it), next(it)
    lamp_ref, gsub_ref, gq_ref, gk_ref, gvb_ref = next(it), next(it), next(it), next(it), next(it)
    ws_ref, bst_ref, wd_ref, sd_ref = next(it), next(it), next(it), next(it)
    mix_ref = next(it)
    kas_ref, vas_ref, kcn_ref, vcs_ref = (None,) * 4 if n_cache else (next(it), next(it), next(it), next(it))
    ka_s, va_s, kc_s, vc_s = next(it), next(it), next(it), next(it)

    j = pl.program_id(1)
    row0 = pl.multiple_of(j * qb, qb)
    n_heads = qa_ref.shape[1] // LANES
    n_kv = kc_s.shape[1] // LANES
    dk_a = LANES // 2

    @pl.when(j == 0)
    def _():
        prep = 256

        def put_values(dst, rows, b, v):
            dst[rows, 2 * b * LANES:(2 * b + 1) * LANES] = v.astype(BF16)
            dst[rows, (2 * b + 1) * LANES:(2 * b + 2) * LANES] = jnp.ones(v.shape, BF16)

        for c in range(seq // prep):
            rows = slice(c * prep, (c + 1) * prep)
            for b in range(n_heads):
                cols = slice(b * LANES, (b + 1) * LANES)
                k = ka_ref[rows, cols]
                if rope:
                    k = _rope(k, cos_a_ref[rows, :], sin_a_ref[rows, :], dk_a // 4)
                ka_s[rows, cols] = k.astype(BF16)
                v = va_ref[rows, cols]
                put_values(va_s, rows, b, v)
                if kas_ref is not None:
                    kas_ref[rows, cols] = k
                    vas_ref[rows, cols] = v
            for b in range(n_kv):
                cols = slice(b * LANES, (b + 1) * LANES)
                k = _rms_lanes(kv_ref[rows, cols], gk_ref[...])
                v = kv_ref[rows, (n_kv + b) * LANES:(n_kv + b + 1) * LANES]
                if kcn_ref is not None:
                    kcn_ref[rows, cols] = k
                    vcs_ref[rows, cols] = v
                if rope:
                    k = _rope(k, cos_c_ref[rows, :], sin_c_ref[rows, :], LANES // 4)
                kc_s[rows, cols] = k.astype(BF16)
                put_values(vc_s, rows, b, v)
        if n_cache:
            tail = slice(seq, seq + n_cache)
            ka_s[tail, :] = cka_ref[0].astype(BF16)
            kc_s[tail, :] = ckc_ref[0].astype(BF16)
            for b in range(n_heads):
                put_values(va_s, tail, b, cva_ref[0, :, b * LANES:(b + 1) * LANES])
            for b in range(n_kv):
                put_values(vc_s, tail, b, cvc_ref[0, :, b * LANES:(b + 1) * LANES])

    lane = lax.broadcasted_iota(jnp.int32, (qb, LANES), 1)

    lp = lamp_ref[...]
    lam = (jnp.exp(jnp.sum(lp[0:1] * lp[1:2], axis=-1, keepdims=True))
           - jnp.exp(jnp.sum(lp[2:3] * lp[3:4], axis=-1, keepdims=True)) + lam0)
    scale_a = dk_a ** -0.5
    if rope:
        cos_a = cos_a_ref[pl.ds(row0, qb), :]
        sin_a = sin_a_ref[pl.ds(row0, qb), :]
        cos_c = cos_c_ref[pl.ds(row0, qb), :]
        sin_c = sin_c_ref[pl.ds(row0, qb), :]
    for h in range(n_heads):
        cols = slice(h * LANES, (h + 1) * LANES)
        q = qa_ref[:, cols]
        if rope:
            q = _rope(q, cos_a, sin_a, dk_a // 4)
        q = q * scale_a
        k = ka_s[:, cols]
        v1 = va_s[:, 2 * h * LANES:(2 * h + 2) * LANES]
        q0 = jnp.where(lane < dk_a, q, 0.0).astype(BF16)
        q1 = jnp.where(lane >= dk_a, q, 0.0).astype(BF16)
        o0 = _dot(_exp_scores(_dot_nt(q0, k)), v1)
        o1 = _dot(_exp_scores(_dot_nt(q1, k)), v1)
        o = o0[:, :LANES] * (1.0 / o0[:, LANES:LANES + 1]) - o1[:, :LANES] * (lam / o1[:, LANES:LANES + 1])
        mix_ref[:, cols] = (_rms_lanes(o, gsub_ref[...]) * (1.0 - lam0)).astype(BF16)

    off_b = n_heads * LANES
    for g in range(ub_ref.shape[1] // LANES):
        cols = slice(g * LANES, (g + 1) * LANES)
        vn = _rms_lanes(vb_ref[:, cols], gvb_ref[:, cols]).astype(BF16)
        w_s = ws_ref[g].astype(BF16)
        for c in range(qb // CHUNK):
            rows = slice(c * CHUNK, (c + 1) * CHUNK)
            mixed = _dot(w_s, vn[rows, :]) + bst_ref[:, g:g + 1]
            mix_ref[rows, off_b + g * LANES: off_b + (g + 1) * LANES] = (ub_ref[rows, cols] * mixed).astype(BF16)

    off_c = off_b + ub_ref.shape[1]
    scale_c = LANES ** -0.5
    for h in range(n_heads):
        cols = slice(h * LANES, (h + 1) * LANES)
        q = _rms_lanes(qc_ref[:, cols], gq_ref[...])
        if rope:
            q = _rope(q, cos_c, sin_c, LANES // 4)
        kv = h // (n_heads // n_kv)
        e = _exp_scores(_dot_nt((q * scale_c).astype(BF16), kc_s[:, kv * LANES:(kv + 1) * LANES]))
        o = _dot(e, vc_s[:, 2 * kv * LANES:(2 * kv + 2) * LANES])
        mix_ref[:, off_c + h * LANES: off_c + (h + 1) * LANES] = (
            o[:, :LANES] * (1.0 / o[:, LANES:LANES + 1])).astype(BF16)

    off_d = off_c + qc_ref.shape[1]
    win = min(qb + 2 * CHUNK, seq)
    start = pl.multiple_of(jnp.clip(row0 - CHUNK, 0, seq - win), CHUNK)
    t = row0 + lax.broadcasted_iota(jnp.int32, (qb, win), 0)
    col = start + lax.broadcasted_iota(jnp.int32, (qb, win), 1)
    t1 = row0 + lax.broadcasted_iota(jnp.int32, (qb, 1), 0)
    for g, w in enumerate(POOL_WINDOWS):
        cols = slice(g * LANES, (g + 1) * LANES)
        left, right = w // 2, w - 1 - w // 2
        band = jnp.where((col >= t - left) & (col <= t + right), 1.0, 0.0).astype(BF16)
        cnt = (jnp.minimum(t1 + right + 1, seq) - jnp.maximum(t1 - left, 0)).astype(F32)
        xw = pd_ref[pl.ds(start, win), cols]
        hi = xw.astype(BF16)
        lo = (xw - hi.astype(F32)).astype(BF16)
        wsum = _dot(band, hi) + _dot(band, lo)
        xg = pd_ref[pl.ds(row0, qb), cols]
        pooled = wsum / cnt - xg
        y = _dot(pooled.astype(BF16), wd_ref[g].astype(BF16)) * sd_ref[:, cols]
        mix_ref[:, off_d + g * LANES: off_d + (g + 1) * LANES] = y.astype(BF16)


def _mixer(proj, l, n_seq, seq, qb, lam0, params, caches=None, rope_tabs=None):
    T, PW = proj.shape
    GW = PW // 8
    n_qb = seq // qb
    n_cache = 0 if caches is None else caches[0].shape[2]
    n_keys = seq + n_cache

    def qspec(col):
        return pl.BlockSpec((qb, GW), lambda s, j: (s * n_qb + j, col))

    def sspec(col):
        return pl.BlockSpec((seq, GW), lambda s, j: (s, col))

    def full(shape):
        nd = len(shape)
        return pl.BlockSpec(shape, lambda s, j: (0,) * nd)

    in_specs = [qspec(0), qspec(3), qspec(4), qspec(5), sspec(1), sspec(2), sspec(6), sspec(7)]
    args = [proj] * 8
    if caches is not None:
        for c in caches:
            in_specs.append(pl.BlockSpec((None, 1, c.shape[2], c.shape[3]), lambda s, j: (s, l, 0, 0)))
            args.append(c)
    if rope_tabs is not None:
        for tab in rope_tabs:
            in_specs.append(full(tab.shape))
            args.append(tab)
    for p in params:
        in_specs.append(full(p.shape))
        args.append(p)

    MW = 4 * GW
    mix_spec = pl.BlockSpec((qb, MW), lambda s, j: (s * n_qb + j, 0))
    if caches is None:
        state_w = (GW, GW, GW // 2, GW // 2)
        out_shape = (jax.ShapeDtypeStruct((n_seq * seq, MW), BF16),
                     *(jax.ShapeDtypeStruct((n_seq * seq, w), F32) for w in state_w))
        out_specs = (mix_spec, *(pl.BlockSpec((seq, w), lambda s, j: (s, 0)) for w in state_w))
    else:
        out_shape = jax.ShapeDtypeStruct((n_seq * seq, MW), BF16)
        out_specs = mix_spec

    kern = functools.partial(_mixer_kernel, seq=seq, qb=qb, n_cache=n_cache, lam0=lam0,
                             rope=rope_tabs is not None)
    return pl.pallas_call(
        kern,
        out_shape=out_shape,
        grid=(n_seq, n_qb),
        in_specs=in_specs,
        out_specs=out_specs,
        scratch_shapes=[pltpu.VMEM((n_keys, GW), BF16), pltpu.VMEM((n_keys, 2 * GW), BF16),
                        pltpu.VMEM((n_keys, GW // 2), BF16), pltpu.VMEM((n_keys, GW), BF16)],
        compiler_params=_cparams(("arbitrary", "arbitrary")),
        name="mixer_latent" if caches is not None else "mixer_context",
    )(*args)


def _outproj_kernel(mix_c_ref, mix_s_ref, w_ref, x_c_ref, x_s_ref, mods_ref, o_ref, *, gate_row, n_ctx_tiles):
    tn = o_ref.shape[1]
    gcols = pl.ds(pl.multiple_of(pl.program_id(1) * tn, tn), tn)
    gate = mods_ref[0, gate_row:gate_row + 1, gcols]
    w = w_ref[0]
    is_ctx = pl.program_id(0) < n_ctx_tiles

    @pl.when(is_ctx)
    def _():
        o_ref[...] = x_c_ref[...] + gate * _dot(mix_c_ref[...], w)

    @pl.when(jnp.logical_not(is_ctx))
    def _():
        o_ref[...] = x_s_ref[...] + gate * _dot(mix_s_ref[...], w)


def _outproj(mix_c, mix_s, w_out, x_c, x_s, mods, l, n_ctx_rows, dec_seq):
    T = mix_c.shape[0] + mix_s.shape[0]
    D = x_c.shape[1]
    MW = mix_c.shape[1]
    tm, tn = TM_DENSE, TN_OUTPROJ
    nct = n_ctx_rows // tm
    kern = functools.partial(_outproj_kernel, gate_row=2, n_ctx_tiles=nct)
    return pl.pallas_call(
        kern,
        out_shape=jax.ShapeDtypeStruct((T, D), F32),
        grid=(T // tm, D // tn),
        in_specs=[
            pl.BlockSpec((tm, MW), lambda m, n: (jnp.minimum(m, nct - 1), 0)),
            pl.BlockSpec((tm, MW), lambda m, n: (jnp.maximum(m - nct, 0), 0)),
            pl.BlockSpec((1, MW, tn), lambda m, n: (l, 0, n)),
            pl.BlockSpec((tm, tn), lambda m, n: (jnp.minimum(m, nct - 1), jnp.where(m < nct, n, D // tn - 1))),
            pl.BlockSpec((tm, tn), lambda m, n: (jnp.maximum(m - nct, 0), jnp.where(m < nct, 0, n))),
            pl.BlockSpec((1, N_MOD, D), lambda m, n: (_cond_index(m, tm, n_ctx_rows, dec_seq), 0, 0)),
        ],
        out_specs=pl.BlockSpec((tm, tn), lambda m, n: (m, n)),
        compiler_params=_cparams(("arbitrary", "arbitrary")),
        name="outproj",
    )(mix_c, mix_s, w_out, x_c, x_s, mods)


def _split_bf16(x):
    hi = x.astype(BF16)
    return hi, (x - hi.astype(F32)).astype(BF16)


def _router_kernel(x_ref, g_ref, mods_ref, wr_ref, br_ref, h_ref, idx_ref, gate_ref, *, n_experts):
    m = mods_ref[0]
    h = _rms_lanes(x_ref[...], g_ref[...]) * (1.0 + m[4:5]) + m[3:4]
    h_ref[...] = h
    h_hi, h_lo = _split_bf16(h)
    w_hi, w_lo = _split_bf16(wr_ref[...])
    logits = _dot(h_hi, w_hi) + _dot(h_hi, w_lo) + _dot(h_lo, w_hi) + br_ref[...]
    lane = lax.broadcasted_iota(jnp.int32, logits.shape, 1)
    lane_f = lane.astype(F32)
    neg = jnp.float32(-jnp.inf)
    logits = jnp.where(lane < n_experts, logits, neg)
    idx_out = jnp.zeros(logits.shape, F32)
    val_out = jnp.zeros(logits.shape, F32)
    top = None
    denom = None
    for k in range(TOP_K):
        v = jnp.max(logits, axis=-1, keepdims=True)
        i = jnp.min(jnp.where(logits == v, lane_f, float(LANES)), axis=-1, keepdims=True)
        if k == 0:
            top = v
        e = jnp.exp(v - top)
        denom = e if denom is None else denom + e
        idx_out = jnp.where(lane == k, i, idx_out)
        val_out = jnp.where(lane == k, e, val_out)
        logits = jnp.where(lane_f == i, neg, logits)
    idx_ref[...] = idx_out.astype(jnp.int32)
    gate_ref[...] = val_out / denom


def _router(x, g, mods, w_router_l, b_router_l, n_ctx_rows, dec_seq):
    T, D = x.shape
    E = w_router_l.shape[1]
    tm = TM_ROUTER
    wr = jnp.zeros((D, LANES), F32).at[:, :E].set(w_router_l)
    br = jnp.zeros((1, LANES), F32).at[0, :E].set(b_router_l)
    kern = functools.partial(_router_kernel, n_experts=E)
    return pl.pallas_call(
        kern,
        out_shape=(jax.ShapeDtypeStruct((T, D), F32),
                   jax.ShapeDtypeStruct((T, LANES), jnp.int32),
                   jax.ShapeDtypeStruct((T, LANES), F32)),
        grid=(T // tm,),
        in_specs=[
            pl.BlockSpec((tm, D), lambda m: (m, 0)),
            pl.BlockSpec((1, D), lambda m: (0, 0)),
            pl.BlockSpec((1, N_MOD, D), lambda m: (_cond_index(m, tm, n_ctx_rows, dec_seq), 0, 0)),
            pl.BlockSpec((D, LANES), lambda m: (0, 0)),
            pl.BlockSpec((1, LANES), lambda m: (0, 0)),
        ],
        out_specs=(pl.BlockSpec((tm, D), lambda m: (m, 0)),
                   pl.BlockSpec((tm, LANES), lambda m: (m, 0)),
                   pl.BlockSpec((tm, LANES), lambda m: (m, 0))),
        compiler_params=_cparams(("arbitrary",)),
        name="norm_router",
    )(x, g, mods, wr, br)


def _route_plan(top_idx, n_experts, n_sb):
    T, K = top_idx.shape
    flat_e = top_idx.reshape(-1)
    onehot = (flat_e[:, None] == jnp.arange(n_experts, dtype=jnp.int32)[None, :]).astype(jnp.int32)
    csum = jnp.cumsum(onehot, axis=0)
    counts = csum[-1]
    nblk = (counts + MOE_ROWS - 1) // MOE_ROWS
    blk_start = jnp.cumsum(nblk) - nblk
    pos = jnp.sum(onehot * (csum - 1 + (blk_start * MOE_ROWS)[None, :]), axis=1)
    nsb = (nblk + SB_BLOCKS - 1) // SB_BLOCKS
    sb_end = jnp.cumsum(nsb)
    s = jnp.arange(n_sb, dtype=jnp.int32)
    total = sb_end[-1]
    sc = jnp.minimum(s, total - 1)
    e = jnp.sum((sb_end[None, :] <= sc[:, None]).astype(jnp.int32), axis=1)
    pick = (e[:, None] == jnp.arange(n_experts, dtype=jnp.int32)[None, :]).astype(jnp.int32)

    def of_expert(v):
        return jnp.sum(pick * v[None, :], axis=1)

    k = sc - (of_expert(sb_end) - of_expert(nsb))
    sb_blk0 = (of_expert(blk_start) + k * SB_BLOCKS).astype(jnp.int32)
    sb_nblk = jnp.where(s < total, jnp.minimum(SB_BLOCKS, of_expert(nblk) - k * SB_BLOCKS), 0).astype(jnp.int32)
    n_used = jnp.sum(nblk).astype(jnp.int32).reshape(1)
    plan = (e, sb_blk0, sb_nblk, n_used)
    return pos.reshape(T, K).astype(jnp.int32), plan, (blk_start.astype(jnp.int32), counts.astype(jnp.int32))


def _dispatch_kernel(start_ref, cnt_ref, n_used_ref, pos_ref, h_ref, xs_hbm, ring, zeros, sem, zsem):
    i = pl.program_id(0)
    n = pl.num_programs(0)
    tt = h_ref.shape[0]
    slot = i & 1

    ring[slot] = h_ref[...]

    def issue(t, c):
        for k in range(TOP_K):
            p = pos_ref[0, 0, t * TOP_K + k]
            pltpu.make_async_copy(ring.at[slot, pl.ds(t, 1)], xs_hbm.at[pl.ds(p, 1)],
                                  sem.at[slot]).start(priority=k % 2)
        return c

    lax.fori_loop(0, tt, issue, 0, unroll=2)

    def drain(sl):
        for _ in range(TOP_K):
            pltpu.make_async_copy(ring.at[sl], ring.at[sl], sem.at[sl]).wait()

    @pl.when(i == 0)
    def _():
        zeros[...] = jnp.zeros(zeros.shape, F32)

        def pad_copy(row):
            return pltpu.make_async_copy(zeros.at[pl.ds(0, 1)], xs_hbm.at[pl.ds(row, 1)], zsem)

        def tail_copy(b):
            dst = xs_hbm.at[pl.ds(pl.multiple_of(b * MOE_ROWS, MOE_ROWS), MOE_ROWS)]
            return pltpu.make_async_copy(zeros, dst, zsem)

        def per_expert(fn):
            def body(e, c):
                cnt = cnt_ref[e]
                first = start_ref[e] * MOE_ROWS + cnt
                lax.fori_loop(0, (-cnt) & (MOE_ROWS - 1), lambda j, cc: (fn(first + j), cc)[1], 0)
                return c
            lax.fori_loop(0, cnt_ref.shape[0], body, 0)

        def per_tail(fn):
            lax.fori_loop(n_used_ref[0], xs_hbm.shape[0] // MOE_ROWS, lambda b, cc: (fn(b), cc)[1], 0)

        per_expert(lambda row: pad_copy(row).start())
        per_tail(lambda b: tail_copy(b).start())
        per_expert(lambda row: pad_copy(row).wait())
        per_tail(lambda b: tail_copy(b).wait())

    @pl.when(i > 0)
    def _():
        drain(1 - slot)

    @pl.when(i == n - 1)
    def _():
        drain(slot)


def _dispatch(h, pos, pad_info, n_used, n_slots):
    T, D = h.shape
    blk_start, counts = pad_info
    tt = TT_DISPATCH
    grid_spec = pltpu.PrefetchScalarGridSpec(
        num_scalar_prefetch=3,
        grid=(T // tt,),
        in_specs=[
            pl.BlockSpec((1, 1, tt * TOP_K), lambda i, s, c, u: (i, 0, 0), memory_space=pltpu.SMEM),
            pl.BlockSpec((tt, D), lambda i, s, c, u: (i, 0)),
        ],
        out_specs=pl.BlockSpec(memory_space=pl.ANY),
        scratch_shapes=[pltpu.VMEM((2, tt, D), F32), pltpu.VMEM((MOE_ROWS, D), F32),
                        pltpu.SemaphoreType.DMA((2,)), pltpu.SemaphoreType.DMA],
    )
    return pl.pallas_call(
        _dispatch_kernel,
        out_shape=jax.ShapeDtypeStruct((n_slots, D), F32),
        grid_spec=grid_spec,
        compiler_params=_cparams(("arbitrary",)),
        name="moe_dispatch",
    )(blk_start, counts, n_used, pos.reshape(T // tt, 1, tt * TOP_K), h)


def _moe_kernel(sb_e_ref, sb_blk0_ref, sb_nblk_ref, n_used_ref, xs_hbm, wg_ref, bg_ref, wu_ref, bu_ref,
                wd_ref, bd_ref, y_hbm, x_s, y_s, stage, sem_in, sem_out):
    s = pl.program_id(0)
    f = pl.program_id(1)
    n_s = pl.num_programs(0)
    n_f = pl.num_programs(1)
    nblk = sb_nblk_ref[s]
    blk0 = sb_blk0_ref[s]
    tf = wg_ref.shape[3]
    n_stage = stage.shape[0]

    def rows_of(blk, nb):
        return pl.ds(pl.multiple_of(blk * MOE_ROWS, MOE_ROWS), nb * MOE_ROWS)

    def in_copy(first_blk, r):
        return pltpu.make_async_copy(xs_hbm.at[rows_of(first_blk + r, 1)], stage.at[r & (n_stage - 1)],
                                     sem_in.at[r & (n_stage - 1)])

    def out_copy(first_blk, r):
        return pltpu.make_async_copy(y_s.at[rows_of(r, 1)], y_hbm.at[rows_of(first_blk + r, 1)], sem_out)

    def for_blocks(n, fn):
        def body(r, c):
            fn(r)
            return c
        lax.fori_loop(0, n, body, 0)

    def prefetch(first_blk, n):
        for_blocks(jnp.minimum(n, n_stage), lambda r: in_copy(first_blk, r).start())

    def ffn(blk, nb, first):
        rows = rows_of(blk, nb)
        if first:
            for i in range(nb):
                r = blk + i
                in_copy(blk0, r).wait()
                x_s[rows_of(r, 1), :] = stage[r & (n_stage - 1)].astype(BF16)

                @pl.when(r + n_stage < nblk)
                def _(r=r):
                    in_copy(blk0, r + n_stage).start()

        wgu = jnp.concatenate([wg_ref[0, 0].astype(BF16), wu_ref[0, 0].astype(BF16)], axis=1)
        gu = _dot(x_s[rows, :], wgu)
        g = jnp.minimum(gu[:, :tf] + bg_ref[0, 0], SWIGLU_LIMIT)
        u = jnp.clip(gu[:, tf:] + bu_ref[0, 0], -SWIGLU_LIMIT, SWIGLU_LIMIT)
        act = (u + 1.0) * g * jax.nn.sigmoid(SWIGLU_ALPHA * g)
        y = _dot(act.astype(BF16), wd_ref[0, 0].astype(BF16))
        if first:
            y_s[rows, :] = y + bd_ref[0, 0]
        else:
            y_s[rows, :] += y

    def all_chunks(first):
        n_big = lax.shift_right_logical(nblk, FFN_CHUNK_LOG2)
        for_blocks(n_big, lambda c: ffn(c * FFN_CHUNK, FFN_CHUNK, first))
        done = n_big * FFN_CHUNK
        nb = FFN_CHUNK // 2
        while nb >= 1:
            @pl.when((nblk & nb) != 0)
            def _(done=done, nb=nb):
                ffn(done, nb, first)
            done = done + (nblk & nb)
            nb //= 2

    @pl.when(nblk > 0)
    def _():
        @pl.when(f == 0)
        def _():
            @pl.when(s == 0)
            def _():
                prefetch(blk0, nblk)

            @pl.when(s > 0)
            def _():
                for_blocks(sb_nblk_ref[s - 1], lambda r: out_copy(sb_blk0_ref[s - 1], r).wait())

            all_chunks(True)

        @pl.when(f > 0)
        def _():
            all_chunks(False)

        @pl.when(f == n_f - 1)
        def _():
            for_blocks(nblk, lambda r: out_copy(blk0, r).start())
            nxt = jnp.minimum(s + 1, n_s - 1)
            has_next = jnp.logical_and(s + 1 < n_s, sb_nblk_ref[nxt] > 0)

            @pl.when(has_next)
            def _():
                prefetch(sb_blk0_ref[nxt], sb_nblk_ref[nxt])

            @pl.when(jnp.logical_not(has_next))
            def _():
                for_blocks(nblk, lambda r: out_copy(blk0, r).wait())

    @pl.when(jnp.logical_and(s == pl.num_programs(0) - 1, f == n_f - 1))
    def _():
        y_s[0:MOE_ROWS, :] = jnp.zeros((MOE_ROWS, y_s.shape[1]), F32)

        def tail_copy(r):
            dst = y_hbm.at[pl.ds(pl.multiple_of(r * MOE_ROWS, MOE_ROWS), MOE_ROWS)]
            return pltpu.make_async_copy(y_s.at[0:MOE_ROWS], dst, sem_out)

        def start(r, c):
            tail_copy(r).start()
            return c

        def wait(r, c):
            tail_copy(r).wait()
            return c

        lax.fori_loop(n_used_ref[0], y_hbm.shape[0] // MOE_ROWS, start, 0)
        lax.fori_loop(n_used_ref[0], y_hbm.shape[0] // MOE_ROWS, wait, 0)


def _moe_ffn(xs, plan, w_gate, b_gate, w_up, b_up, w_down, b_down, l, n_sb):
    sb_e, sb_blk0, sb_nblk, n_used = plan
    n_slots, D = xs.shape
    L, E, _, F = w_gate.shape
    tf = TF_MOE
    rows = SB_BLOCKS * MOE_ROWS
    n_f = F // tf

    def ftile(s, f, n):
        return jnp.where(n[s] > 0, f, n_f - 1)

    grid_spec = pltpu.PrefetchScalarGridSpec(
        num_scalar_prefetch=4,
        grid=(n_sb, n_f),
        in_specs=[
            pl.BlockSpec(memory_space=pl.ANY),
            pl.BlockSpec((1, 1, D, tf), lambda s, f, e, b, n, u: (l, e[s], 0, ftile(s, f, n))),
            pl.BlockSpec((1, 1, 1, tf), lambda s, f, e, b, n, u: (l, e[s], 0, ftile(s, f, n))),
            pl.BlockSpec((1, 1, D, tf), lambda s, f, e, b, n, u: (l, e[s], 0, ftile(s, f, n))),
            pl.BlockSpec((1, 1, 1, tf), lambda s, f, e, b, n, u: (l, e[s], 0, ftile(s, f, n))),
            pl.BlockSpec((1, 1, tf, D), lambda s, f, e, b, n, u: (l, e[s], ftile(s, f, n), 0)),
            pl.BlockSpec((1, 1, 1, D), lambda s, f, e, b, n, u: (l, e[s], 0, 0)),
        ],
        out_specs=pl.BlockSpec(memory_space=pl.ANY),
        scratch_shapes=[
            pltpu.VMEM((rows, D), BF16),
            pltpu.VMEM((rows, D), F32),
            pltpu.VMEM((XS_STAGES, MOE_ROWS, D), F32),
            pltpu.SemaphoreType.DMA((XS_STAGES,)),
            pltpu.SemaphoreType.DMA,
        ],
    )
    return pl.pallas_call(
        _moe_kernel,
        out_shape=jax.ShapeDtypeStruct((n_slots, D), F32),
        grid_spec=grid_spec,
        compiler_params=_cparams(("arbitrary", "arbitrary")),
        name="moe_ffn",
    )(sb_e, sb_blk0, sb_nblk, n_used, xs, w_gate, b_gate.reshape(L, E, 1, F), w_up, b_up.reshape(L, E, 1, F),
      w_down, b_down.reshape(L, E, 1, D))


def _combine_kernel(pos_ref, pos_next_ref, y_hbm, gates_ref, x_ref, mods_ref, g_ref, o_c_ref, o_s_ref, stage, sem,
                    *, final_norm, n_ctx_tiles):
    i = pl.program_id(0)
    n_steps = pl.num_programs(0)
    n = x_ref.shape[0]
    slot = i & 1

    def fetch(p_ref, sl):
        def issue(t, c):
            for k in range(TOP_K):
                p = p_ref[0, 0, t * TOP_K + k]
                pltpu.make_async_copy(y_hbm.at[pl.ds(p, 1)], stage.at[sl, k, pl.ds(t, 1)],
                                      sem.at[sl]).start(priority=k % 2)
            return c
        lax.fori_loop(0, n, issue, 0, unroll=2)

    @pl.when(i == 0)
    def _():
        fetch(pos_ref, slot)

    @pl.when(i + 1 < n_steps)
    def _():
        fetch(pos_next_ref, 1 - slot)

    for k in range(TOP_K):
        pltpu.make_async_copy(stage.at[slot, k], stage.at[slot, k], sem.at[slot]).wait()
    gates = gates_ref[...]
    acc = gates[:, 0:1] * stage[slot, 0]
    for k in range(1, TOP_K):
        acc = acc + gates[:, k:k + 1] * stage[slot, k]
    res = x_ref[...] + mods_ref[0][5:6] * acc
    res = _rms_lanes(res, g_ref[...]) if final_norm else res

    @pl.when(i < n_ctx_tiles)
    def _():
        o_c_ref[...] = res

    @pl.when(i >= n_ctx_tiles)
    def _():
        o_s_ref[...] = res


def _combine(y_slots, pos, gates, x, mods, g_final, final_norm, n_ctx_rows, dec_seq):
    T, D = x.shape
    cb = CB_COMBINE
    n_steps = T // cb
    nct = n_ctx_rows // cb
    return pl.pallas_call(
        functools.partial(_combine_kernel, final_norm=final_norm, n_ctx_tiles=nct),
        out_shape=(jax.ShapeDtypeStruct((n_ctx_rows, D), F32), jax.ShapeDtypeStruct((T - n_ctx_rows, D), F32)),
        grid=(n_steps,),
        in_specs=[
            pl.BlockSpec((1, 1, cb * TOP_K), lambda i: (i, 0, 0), memory_space=pltpu.SMEM),
            pl.BlockSpec((1, 1, cb * TOP_K), lambda i: (jnp.minimum(i + 1, n_steps - 1), 0, 0),
                         memory_space=pltpu.SMEM),
            pl.BlockSpec(memory_space=pl.ANY),
            pl.BlockSpec((cb, LANES), lambda i: (i, 0)),
            pl.BlockSpec((cb, D), lambda i: (i, 0)),
            pl.BlockSpec((1, N_MOD, D), lambda i: (_cond_index(i, cb, n_ctx_rows, dec_seq), 0, 0)),
            pl.BlockSpec((1, D), lambda i: (0, 0)),
        ],
        out_specs=(pl.BlockSpec((cb, D), lambda i: (jnp.minimum(i, nct - 1), 0)),
                   pl.BlockSpec((cb, D), lambda i: (jnp.maximum(i - nct, 0), 0))),
        scratch_shapes=[pltpu.VMEM((2, TOP_K, cb, D), F32), pltpu.SemaphoreType.DMA((2,))],
        compiler_params=_cparams(("arbitrary",)),
        name="moe_combine",
    )(pos.reshape(n_steps, 1, cb * TOP_K), pos.reshape(n_steps, 1, cb * TOP_K), y_slots, gates, x, mods, g_final)


def _rope_tables(n, head_dim):
    quarter = head_dim // 4
    inv = ROPE_THETA ** (-np.arange(quarter, dtype=np.float64) / quarter)
    pos = np.arange(n)
    row = (pos // GRID_W)[:, None] * inv
    col = (pos % GRID_W)[:, None] * inv
    cos = np.concatenate([np.cos(row), np.cos(row), np.cos(col), np.cos(col)], axis=1)
    sin = np.concatenate([-np.sin(row), np.sin(row), -np.sin(col), np.sin(col)], axis=1)
    reps = LANES // head_dim
    return (jnp.asarray(np.tile(cos, (1, reps)), dtype=F32), jnp.asarray(np.tile(sin, (1, reps)), dtype=F32))


def kernel(x_prompt, x_sample, c, cache_k_a, cache_v_a, cache_k_c, cache_v_c, c_ctx, w_mod, b_mod, g_norm1, g_norm2, w_in, lam_q1, lam_k1, lam_q2, lam_k2, g_subln_a, g_q_c, g_k_c, g_v_b, w_s_b, b_s_b, w_d, s_d, w_out, w_router, b_router, w_gate, b_gate, w_up, b_up, w_down, b_down, g_final):
    B, S, D = x_prompt.shape
    DB, DS, _ = x_sample.shape
    L = w_mod.shape[0]
    P = cache_k_a.shape[2]
    E = w_router.shape[2]
    n_ctx = B * S
    T = n_ctx + DB * DS
    GW = w_out.shape[1] // 4
    dk_a = cache_k_a.shape[-1]
    hd_c = cache_k_c.shape[-1]

    x_c = x_prompt.reshape(n_ctx, D)
    x_s = x_sample.reshape(DB * DS, D)
    cond8 = jnp.zeros((8, D), F32).at[0].set(c_ctx).at[1:1 + DB].set(c)
    mod = _modulation(cond8, w_mod, b_mod).reshape(L, 8, N_MOD, D)

    cos_a, sin_a = _rope_tables(DS, dk_a)
    cos_c, sin_c = _rope_tables(DS, hd_c)
    caches_all = (cache_k_a.reshape(DB, L, P, GW), cache_v_a.reshape(DB, L, P, GW),
                  cache_k_c.reshape(DB, L, P, GW // 2), cache_v_c.reshape(DB, L, P, GW // 2))

    w_in = _to_bf16(w_in)
    w_out = _to_bf16(w_out)
    n_slots_max = -(-(T * TOP_K + E * (MOE_ROWS - 1)) // MOE_ROWS)
    n_sb = E + n_slots_max // SB_BLOCKS

    states = []
    for l in range(L):
        mods = mod[l]
        lam0 = 0.8 - 0.6 * math.exp(-0.3 * l)
        g1 = g_norm1[l][None, :]
        proj_c = _inproj(x_c, 0, g1, mods, w_in, l, n_ctx, DS)
        proj_s = _inproj(x_s, n_ctx, g1, mods, w_in, l, n_ctx, DS)
        params = (jnp.stack([lam_q1[l], lam_k1[l], lam_q2[l], lam_k2[l]]), g_subln_a[l][None, :],
                  g_q_c[l][None, :], g_k_c[l][None, :], g_v_b[l][None, :], w_s_b[l], b_s_b[l].T,
                  w_d[l], s_d[l][None, :])
        mix_c, *state = _mixer(proj_c, l, B, S, S, lam0, params)
        mix_s = _mixer(proj_s, l, DB, DS, 2 * CHUNK, lam0, params, caches=caches_all,
                       rope_tabs=(cos_a, sin_a, cos_c, sin_c))
        states.append(state)
        x = _outproj(mix_c, mix_s, w_out, x_c, x_s, mods, l, n_ctx, DS)
        h, top_idx, gates = _router(x, g_norm2[l][None, :], mods, w_router[l], b_router[l], n_ctx, DS)
        pos, plan, pad_info = _route_plan(top_idx[:, :TOP_K], E, n_sb)
        xs = _dispatch(h, pos, pad_info, plan[3], n_slots_max * MOE_ROWS)
        y_slots = _moe_ffn(xs, plan, w_gate, b_gate, w_up, b_up, w_down, b_down, l, n_sb)
        x_c, x_s = _combine(y_slots, pos, gates, x, mods, g_final[None, :], l == L - 1, n_ctx, DS)

    y_prompt = x_c.reshape(B, S, D)
    y_sample = x_s.reshape(DB, DS, D)
    h_a = GW // (2 * dk_a)
    new_k_a = jnp.stack([st[0].reshape(B, S, h_a, 2, dk_a) for st in states], axis=1)
    new_v_a = jnp.stack([st[1].reshape(B, S, h_a, 2 * dk_a) for st in states], axis=1)
    new_k_c = jnp.stack([st[2].reshape(B, S, GW // 2 // hd_c, hd_c) for st in states], axis=1)
    new_v_c = jnp.stack([st[3].reshape(B, S, GW // 2 // hd_c, hd_c) for st in states], axis=1)
    return (y_prompt, y_sample, new_k_a, new_v_a, new_k_c, new_v_c)
```
